```python
import math
import jax, jax.numpy as jnp
from jax import lax
import numpy as np

D_MODEL = 1024
BATCH = 8
SEQ = 4096
DEPTH = 4
DEC_BATCH = 2
DEC_SEQ = 8192
PAST_LEN = 128

GRID_W = 64
Q_BLOCK = 128
DIFF_HEADS = 8
DIFF_HEAD_DIM = 64
DIFF_ROT_DIM = DIFF_HEAD_DIM // 4
ROPE_THETA = 500000.0
GQA_Q_HEADS = 16
GQA_KV_HEADS = 4
GQA_HEAD_DIM = 64
AXIAL_THETA = 10000.0
D_FF_DENSE = 2816
N_EXPERTS = 8
TOP_K = 2
D_FF_EXPERT = 3584
NORM_EPS = 1e-6
N_DENSE = (DEPTH + 1) // 2
N_MOE = DEPTH // 2
N_MOD = 6

DIFF_QK_W = DIFF_HEADS * 2 * DIFF_HEAD_DIM
DIFF_V_W = DIFF_HEADS * 2 * DIFF_HEAD_DIM
GQA_Q_W = GQA_Q_HEADS * GQA_HEAD_DIM
GQA_KV_W = GQA_KV_HEADS * GQA_HEAD_DIM
GATE_W = 2 * D_MODEL
IN_SPLITS = [DIFF_QK_W,
             2 * DIFF_QK_W,
             2 * DIFF_QK_W + DIFF_V_W,
             2 * DIFF_QK_W + DIFF_V_W + GQA_Q_W,
             2 * DIFF_QK_W + DIFF_V_W + GQA_Q_W + GQA_KV_W,
             2 * DIFF_QK_W + DIFF_V_W + GQA_Q_W + 2 * GQA_KV_W]
IN_WIDTH = 2 * DIFF_QK_W + DIFF_V_W + GQA_Q_W + 2 * GQA_KV_W + GATE_W

kernel_name = "hybrid_diffattn_axialgqa_moe_encoder"


def rmsnorm(x, g):
    xf = x.astype(jnp.float32)
    y = xf * lax.rsqrt(jnp.mean(xf * xf, axis=-1, keepdims=True) + NORM_EPS)
    return (y * g.astype(jnp.float32)).astype(x.dtype)


def apply_rope(x, pos, theta):
    half = x.shape[-1] // 2
    inv_freq = jnp.exp(-math.log(theta) * jnp.arange(half, dtype=jnp.float32) / half)
    ang = pos.astype(jnp.float32)[:, None] * inv_freq[None, :]
    bshape = (1, pos.shape[0]) + (1,) * (x.ndim - 3) + (half,)
    cos = jnp.cos(ang).reshape(bshape).astype(x.dtype)
    sin = jnp.sin(ang).reshape(bshape).astype(x.dtype)
    x1, x2 = x[..., :half], x[..., half:]
    return jnp.concatenate([x1 * cos - x2 * sin, x2 * cos + x1 * sin], axis=-1)


def diff_attention(q, k, v, lam):
    B, S, H, _, d = q.shape
    qb = jnp.moveaxis(q.reshape(B, S // Q_BLOCK, Q_BLOCK, H, 2, d), 1, 0)
    scale = d ** -0.5

    def block(qi):
        s = jnp.einsum('bqhmd,bkhmd->bhmqk', qi, k, preferred_element_type=jnp.float32) * scale
        p = jax.nn.softmax(s, axis=-1)
        a = (p[:, :, 0] - lam * p[:, :, 1]).astype(v.dtype)
        return jnp.einsum('bhqk,bkhe->bqhe', a, v)

    o = lax.map(block, qb)
    return jnp.moveaxis(o, 0, 1).reshape(B, S, H, v.shape[-1])


def gqa_attention(q, k, v):
    B, S, G, R, d = q.shape
    qb = jnp.moveaxis(q.reshape(B, S // Q_BLOCK, Q_BLOCK, G, R, d), 1, 0)
    scale = d ** -0.5

    def block(qi):
        s = jnp.einsum('bqgrd,bkgd->bgrqk', qi, k, preferred_element_type=jnp.float32) * scale
        p = jax.nn.softmax(s, axis=-1).astype(v.dtype)
        return jnp.einsum('bgrqk,bkgd->bqgrd', p, v)

    o = lax.map(block, qb)
    return jnp.moveaxis(o, 0, 1).reshape(B, S, G * R * d)


def mixing(h, pos, row, col, lam_init, w_in, diff_lambda, diff_subln, gqa_q_norm, gqa_k_norm, w_out):
    B, S, _ = h.shape
    proj = h @ w_in
    dq, dk, dv, gq, gk, gv, gl = jnp.split(proj, IN_SPLITS, axis=-1)

    dq = dq.reshape(B, S, DIFF_HEADS, 2, DIFF_HEAD_DIM)
    dk = dk.reshape(B, S, DIFF_HEADS, 2, DIFF_HEAD_DIM)
    dv = dv.reshape(B, S, DIFF_HEADS, 2 * DIFF_HEAD_DIM)
    dq = jnp.concatenate([apply_rope(dq[..., :DIFF_ROT_DIM], pos, ROPE_THETA), dq[..., DIFF_ROT_DIM:]], axis=-1)
    dk = jnp.concatenate([apply_rope(dk[..., :DIFF_ROT_DIM], pos, ROPE_THETA), dk[..., DIFF_ROT_DIM:]], axis=-1)
    lp = diff_lambda.astype(jnp.float32)
    lam = jnp.exp(jnp.sum(lp[0] * lp[1])) - jnp.exp(jnp.sum(lp[2] * lp[3])) + lam_init
    a = diff_attention(dq, dk, dv, lam)
    a = (rmsnorm(a, diff_subln) * (1.0 - lam_init)).reshape(B, S, DIFF_HEADS * 2 * DIFF_HEAD_DIM)

    half = GQA_HEAD_DIM // 2
    gq = rmsnorm(gq.reshape(B, S, GQA_Q_HEADS, GQA_HEAD_DIM), gqa_q_norm)
    gk = rmsnorm(gk.reshape(B, S, GQA_KV_HEADS, GQA_HEAD_DIM), gqa_k_norm)
    gv = gv.reshape(B, S, GQA_KV_HEADS, GQA_HEAD_DIM)
    gq = jnp.concatenate([apply_rope(gq[..., :half], row, AXIAL_THETA), apply_rope(gq[..., half:], col, AXIAL_THETA)], axis=-1)
    gk = jnp.concatenate([apply_rope(gk[..., :half], row, AXIAL_THETA), apply_rope(gk[..., half:], col, AXIAL_THETA)], axis=-1)
    gq = gq.reshape(B, S, GQA_KV_HEADS, GQA_Q_HEADS // GQA_KV_HEADS, GQA_HEAD_DIM)
    b = gqa_attention(gq, gk, gv)

    g = jax.nn.sigmoid(gl.astype(jnp.float32)).astype(h.dtype)
    merged = g[..., :D_MODEL] * a + g[..., D_MODEL:] * b
    return merged @ w_out


def swiglu(h, w_gate, w_up, w_down):
    return (jax.nn.silu(h @ w_gate) * (h @ w_up)) @ w_down


def moe_swiglu(h, router_w, w_gate, w_up, w_down):
    logits = jnp.einsum('bsd,de->bse', h, router_w, preferred_element_type=jnp.float32)
    top_v, top_i = lax.top_k(logits, TOP_K)
    top_w = jax.nn.softmax(top_v, axis=-1)
    gates = jnp.sum(jax.nn.one_hot(top_i, N_EXPERTS, dtype=jnp.float32) * top_w[..., None], axis=-2)
    out = jnp.zeros(h.shape, jnp.float32)
    for e in range(N_EXPERTS):
        out = out + gates[..., e:e + 1] * swiglu(h, w_gate[e], w_up[e], w_down[e]).astype(jnp.float32)
    return out.astype(h.dtype)


def trunk(x, c, w_ada, b_ada, norm1, w_in, diff_lambda, diff_subln, gqa_q_norm, gqa_k_norm, w_out, norm2,
          ffn_w_gate, ffn_w_up, ffn_w_down, router_w, moe_w_gate, moe_w_up, moe_w_down, final_norm):
    B, S, _ = x.shape
    rows = S // GRID_W
    pos = jnp.arange(S, dtype=jnp.int32)
    row = jnp.repeat(jnp.arange(rows, dtype=jnp.int32), GRID_W)
    col = jnp.tile(jnp.arange(GRID_W, dtype=jnp.int32), rows)
    c_act = jax.nn.silu(c)
    for l in range(DEPTH):
        mod = (c_act @ w_ada[l] + b_ada[l])[:, None, :]
        sh1, sc1, g1, sh2, sc2, g2 = jnp.split(mod, N_MOD, axis=-1)
        lam_init = 0.8 - 0.6 * math.exp(-0.3 * l)
        h = rmsnorm(x, norm1[l]) * (1.0 + sc1) + sh1
        x = x + g1 * mixing(h, pos, row, col, lam_init, w_in[l], diff_lambda[l], diff_subln[l],
                            gqa_q_norm[l], gqa_k_norm[l], w_out[l])
        h = rmsnorm(x, norm2[l]) * (1.0 + sc2) + sh2
        if l % 2 == 0:
            y = swiglu(h, ffn_w_gate[l // 2], ffn_w_up[l // 2], ffn_w_down[l // 2])
        else:
            y = moe_swiglu(h, router_w[l // 2], moe_w_gate[l // 2], moe_w_up[l // 2], moe_w_down[l // 2])
        x = x + g2 * y
    return rmsnorm(x, final_norm)


def setup_inputs(seed: int = 0) -> dict:
    key = jax.random.key(seed)
    ks = jax.random.split(key, 24)
    f32 = jnp.float32

    def nrm(k, shape, scale):
        return jax.random.normal(k, shape, f32) * scale

    def gain(k, shape):
        return 1.0 + 0.02 * jax.random.normal(k, shape, f32)

    return {
        "x_prompt": nrm(ks[0], (BATCH, SEQ, D_MODEL), 1.0),
        "x_sample": nrm(ks[1], (DEC_BATCH, DEC_SEQ, D_MODEL), 1.0),
        "c_prompt": nrm(ks[2], (BATCH, D_MODEL), 1.0),
        "c_sample": nrm(ks[3], (DEC_BATCH, D_MODEL), 1.0),
        "w_ada": nrm(ks[4], (DEPTH, D_MODEL, N_MOD * D_MODEL), 0.5 * D_MODEL ** -0.5),
        "b_ada": nrm(ks[5], (DEPTH, N_MOD * D_MODEL), 0.02),
        "norm1": gain(ks[6], (DEPTH, D_MODEL)),
        "w_in": nrm(ks[7], (DEPTH, D_MODEL, IN_WIDTH), D_MODEL ** -0.5),
        "diff_lambda": nrm(ks[8], (DEPTH, 4, DIFF_HEAD_DIM), 0.1),
        "diff_subln": gain(ks[9], (DEPTH, 2 * DIFF_HEAD_DIM)),
        "gqa_q_norm": gain(ks[10], (DEPTH, GQA_HEAD_DIM)),
        "gqa_k_norm": gain(ks[11], (DEPTH, GQA_HEAD_DIM)),
        "w_out": nrm(ks[12], (DEPTH, D_MODEL, D_MODEL), D_MODEL ** -0.5),
        "norm2": gain(ks[13], (DEPTH, D_MODEL)),
        "ffn_w_gate": nrm(ks[14], (N_DENSE, D_MODEL, D_FF_DENSE), D_MODEL ** -0.5),
        "ffn_w_up": nrm(ks[15], (N_DENSE, D_MODEL, D_FF_DENSE), D_MODEL ** -0.5),
        "ffn_w_down": nrm(ks[16], (N_DENSE, D_FF_DENSE, D_MODEL), D_FF_DENSE ** -0.5),
        "router_w": nrm(ks[17], (N_MOE, D_MODEL, N_EXPERTS), D_MODEL ** -0.5),
        "moe_w_gate": nrm(ks[18], (N_MOE, N_EXPERTS, D_MODEL, D_FF_EXPERT), D_MODEL ** -0.5),
        "moe_w_up": nrm(ks[19], (N_MOE, N_EXPERTS, D_MODEL, D_FF_EXPERT), D_MODEL ** -0.5),
        "moe_w_down": nrm(ks[20], (N_MOE, N_EXPERTS, D_FF_EXPERT, D_MODEL), D_FF_EXPERT ** -0.5),
        "final_norm": gain(ks[21], (D_MODEL,)),
    }


def reference(x_prompt, x_sample, c_prompt, c_sample, w_ada, b_ada, norm1, w_in, diff_lambda, diff_subln,
              gqa_q_norm, gqa_k_norm, w_out, norm2, ffn_w_gate, ffn_w_up, ffn_w_down, router_w,
              moe_w_gate, moe_w_up, moe_w_down, final_norm):
    y_prompt = trunk(x_prompt, c_prompt, w_ada, b_ada, norm1, w_in, diff_lambda, diff_subln, gqa_q_norm,
                     gqa_k_norm, w_out, norm2, ffn_w_gate, ffn_w_up, ffn_w_down, router_w,
                     moe_w_gate, moe_w_up, moe_w_down, final_norm)
    y_sample = trunk(x_sample, c_sample, w_ada, b_ada, norm1, w_in, diff_lambda, diff_subln, gqa_q_norm,
                     gqa_k_norm, w_out, norm2, ffn_w_gate, ffn_w_up, ffn_w_down, router_w,
                     moe_w_gate, moe_w_up, moe_w_down, final_norm)
    return (y_prompt, y_sample)
```

```python
import functools
import math

import jax
import jax.numpy as jnp
from jax import lax
from jax.experimental import pallas as pl
from jax.experimental.pallas import tpu as pltpu

F32 = jnp.float32
BF16 = jnp.bfloat16

D_MODEL = 1024
BATCH = 8
SEQ = 4096
DEPTH = 4
DEC_BATCH = 2
DEC_SEQ = 8192
GRID_W = 64
DIFF_HEADS = 8
DIFF_HEAD_DIM = 64
DIFF_ROT_DIM = DIFF_HEAD_DIM // 4
ROPE_THETA = 500000.0
GQA_Q_HEADS = 16
GQA_KV_HEADS = 4
GQA_HEAD_DIM = 64
AXIAL_THETA = 10000.0
D_FF_DENSE = 2816
N_EXPERTS = 8
D_FF_EXPERT = 3584
NORM_EPS = 1e-6
N_MOD = 6

N_PROMPT = BATCH * SEQ
N_SAMPLE = DEC_BATCH * DEC_SEQ
N_TOK = N_PROMPT + N_SAMPLE
MOD_CHUNK = SEQ
N_CHUNKS = N_TOK // MOD_CHUNK

LANES = 128
VMEM_LIMIT = 56 * 1024 * 1024

PROJ_TN = 512
COL_DQ, COL_DK, COL_DV, COL_GQ = 0, 1024, 2048, 3072
COL_GK, COL_GV, COL_GA, COL_GB = 4096, 4608, 5120, 6144
PROJ_W = 7168
N_PROJ_TILES = PROJ_W // PROJ_TN


def _params(sem, vmem=VMEM_LIMIT):
    return pltpu.CompilerParams(dimension_semantics=sem, vmem_limit_bytes=vmem)


def _ada_kernel(c_ref, w_ref, b_ref, o_ref):
    c = c_ref[...]
    ca = c * jax.nn.sigmoid(c)
    o_ref[0] = jnp.dot(ca.astype(BF16), w_ref[0].astype(BF16), preferred_element_type=F32) + b_ref[0]


def _ada(c_all, w_ada, b_ada):
    rows = c_all.shape[0]
    tn = 1536
    width = N_MOD * D_MODEL
    return pl.pallas_call(
        _ada_kernel,
        grid=(DEPTH, width // tn),
        in_specs=[pl.BlockSpec((rows, D_MODEL), lambda l, j: (0, 0)),
                  pl.BlockSpec((1, D_MODEL, tn), lambda l, j: (l, 0, j)),
                  pl.BlockSpec((1, 1, tn), lambda l, j: (l, 0, j))],
        out_specs=pl.BlockSpec((1, rows, tn), lambda l, j: (l, 0, j)),
        out_shape=jax.ShapeDtypeStruct((DEPTH, rows, width), F32),
        compiler_params=_params(("arbitrary", "arbitrary")),
        name="ada_mod",
    )(c_all, w_ada, b_ada.reshape(DEPTH, 1, width))


def _modulated_norm(x, gain, shift, scale):
    ms = jnp.mean(x * x, axis=-1, keepdims=True)
    y = x * lax.rsqrt(ms + NORM_EPS) * gain
    return y * (1.0 + scale) + shift


def _rotate(x, c, sm, sp, shift):
    return x * c + pltpu.roll(x, LANES - shift, axis=1) * sm + pltpu.roll(x, shift, axis=1) * sp


def _pos_block(i, tm):
    return jnp.where(i < N_PROMPT // tm, i % (SEQ // tm), i % (DEC_SEQ // tm))


def _proj_kernel(x_ref, mod_ref, n1_ref, w_ref, tabd_ref, tabg_ref, gain_ref, grp_ref, o_ref, h_scr):
    j = pl.program_id(1)

    @pl.when(j == 0)
    def _():
        h = _modulated_norm(x_ref[...], n1_ref[...], mod_ref[0, 0:1, :], mod_ref[0, 1:2, :])
        h_scr[...] = h.astype(BF16)

    acc = jnp.dot(h_scr[...], w_ref[...], preferred_element_type=F32)
    n_slab = PROJ_TN // LANES

    @pl.when(j < COL_DV // PROJ_TN)
    def _():
        c, sm, sp = tabd_ref[0], tabd_ref[1], tabd_ref[2]
        for s in range(n_slab):
            xs = acc[:, s * LANES:(s + 1) * LANES]
            o_ref[:, s * LANES:(s + 1) * LANES] = _rotate(xs, c, sm, sp, DIFF_ROT_DIM // 2).astype(BF16)

    is_plain = ((j >= COL_DV // PROJ_TN) & (j < COL_GQ // PROJ_TN)) | (j == COL_GV // PROJ_TN)

    @pl.when(is_plain)
    def _():
        o_ref[...] = acc.astype(BF16)

    @pl.when((j >= COL_GQ // PROJ_TN) & (j < COL_GV // PROJ_TN))
    def _():
        sq = acc * acc
        hi = sq.astype(BF16)
        lo = (sq - hi.astype(F32)).astype(BF16)
        ms = (jnp.dot(hi, grp_ref[...], preferred_element_type=F32)
              + jnp.dot(lo, grp_ref[...], preferred_element_type=F32))
        row = jnp.where(j < COL_GK // PROJ_TN, 0, 1)
        gain = gain_ref[pl.ds(row, 1), :]
        y = acc * lax.rsqrt(ms + NORM_EPS) * gain
        c, sm, sp = tabg_ref[0], tabg_ref[1], tabg_ref[2]
        for s in range(n_slab):
            ys = y[:, s * LANES:(s + 1) * LANES]
            o_ref[:, s * LANES:(s + 1) * LANES] = _rotate(ys, c, sm, sp, GQA_HEAD_DIM // 4).astype(BF16)

    @pl.when(j >= COL_GA // PROJ_TN)
    def _():
        o_ref[...] = jax.nn.sigmoid(acc).astype(BF16)


def _proj(x, mod, norm1, w_ext, tab_d, tab_g, gains, grp):
    tm = 1024
    return pl.pallas_call(
        _proj_kernel,
        grid=(N_TOK // tm, N_PROJ_TILES),
        in_specs=[pl.BlockSpec((tm, D_MODEL), lambda i, j: (i, 0)),
                  pl.BlockSpec((1, N_MOD, D_MODEL), lambda i, j: (i * tm // MOD_CHUNK, 0, 0)),
                  pl.BlockSpec((1, D_MODEL), lambda i, j: (0, 0)),
                  pl.BlockSpec((D_MODEL, PROJ_TN), lambda i, j: (0, j)),
                  pl.BlockSpec((3, tm, LANES), lambda i, j: (0, _pos_block(i, tm), 0)),
                  pl.BlockSpec((3, tm, LANES), lambda i, j: (0, _pos_block(i, tm), 0)),
                  pl.BlockSpec((2, PROJ_TN), lambda i, j: (0, 0)),
                  pl.BlockSpec((PROJ_TN, PROJ_TN), lambda i, j: (0, 0))],
        out_specs=pl.BlockSpec((tm, PROJ_TN), lambda i, j: (i, j)),
        out_shape=jax.ShapeDtypeStruct((N_TOK, PROJ_W), BF16),
        scratch_shapes=[pltpu.VMEM((tm, D_MODEL), BF16)],
        compiler_params=_params(("arbitrary", "arbitrary")),
        name="proj_in",
    )(x, mod, norm1, w_ext, tab_d, tab_g, gains, grp)


def _attn_kernel(*refs, diff, lam_init):
    if diff:
        q_ref, k_ref, v_ref, lam_ref, subln_ref, o_ref = refs
    else:
        q_ref, k_ref, v_ref, o_ref = refs
    q = q_ref[...]
    lane = lax.broadcasted_iota(jnp.int32, q.shape, 1)
    low = lane < LANES // 2
    qs = q * jnp.asarray(DIFF_HEAD_DIM ** -0.5, BF16)
    zero = jnp.zeros_like(qs)
    k = k_ref[...]
    v = v_ref[...]
    dn = (((1,), (1,)), ((), ()))

    def softmax_pv(qh):
        s = lax.dot_general(qh, k, dn, preferred_element_type=F32)
        m = jnp.max(s, axis=-1, keepdims=True)
        p = jnp.exp(s - m)
        l = jnp.sum(p, axis=-1, keepdims=True)
        return jnp.dot(p.astype(BF16), v, preferred_element_type=F32) / l

    o_lo = softmax_pv(jnp.where(low, qs, zero))
    o_hi = softmax_pv(jnp.where(low, zero, qs))
    if diff:
        lp = lam_ref[...]
        t1 = jnp.sum(lp[0:1] * lp[1:2], axis=-1, keepdims=True)
        t2 = jnp.sum(lp[2:3] * lp[3:4], axis=-1, keepdims=True)
        lam = jnp.exp(t1) - jnp.exp(t2) + lam_init
        a = o_lo - lam * o_hi
        ms = jnp.mean(a * a, axis=-1, keepdims=True)
        a = a * lax.rsqrt(ms + NORM_EPS) * subln_ref[...] * (1.0 - lam_init)
        o_ref[...] = a.astype(BF16)
    else:
        o_ref[...] = jnp.where(low, o_lo, o_hi).astype(BF16)


def _attention(proj, *, diff, batch, seq, row0, lam_init=0.0, lam=None, subln=None):
    tq = 256
    nq = seq // tq
    q0 = row0 // tq
    s0 = row0 // seq
    n_units = DIFF_HEADS
    if diff:
        qc, kc, vc = COL_DQ // LANES, COL_DK // LANES, COL_DV // LANES
        kmap = lambda b, h, i: (s0 + b, kc + h)
        vmap = lambda b, h, i: (s0 + b, vc + h)
    else:
        qc, kc, vc = COL_GQ // LANES, COL_GK // LANES, COL_GV // LANES
        kmap = lambda b, h, i: (s0 + b, kc + h // 2)
        vmap = lambda b, h, i: (s0 + b, vc + h // 2)
    in_specs = [pl.BlockSpec((tq, LANES), lambda b, h, i: (q0 + b * nq + i, qc + h)),
                pl.BlockSpec((seq, LANES), kmap),
                pl.BlockSpec((seq, LANES), vmap)]
    args = [proj, proj, proj]
    if diff:
        in_specs += [pl.BlockSpec((4, DIFF_HEAD_DIM), lambda b, h, i: (0, 0)),
                     pl.BlockSpec((1, LANES), lambda b, h, i: (0, 0))]
        args += [lam, subln]
    return pl.pallas_call(
        functools.partial(_attn_kernel, diff=diff, lam_init=lam_init),
        grid=(batch, n_units, nq),
        in_specs=in_specs,
        out_specs=pl.BlockSpec((tq, LANES), lambda b, h, i: (b * nq + i, h)),
        out_shape=jax.ShapeDtypeStruct((batch * seq, D_MODEL), BF16),
        compiler_params=_params(("arbitrary", "arbitrary", "arbitrary")),
        name=("diff_attn" if diff else "gqa_attn") + f"_{seq}",
    )(*args)


def _merge_kernel(ga_ref, gb_ref, a_ref, b_ref, x_ref, mod_ref, w_ref, o_ref):
    merged = (ga_ref[...].astype(F32) * a_ref[...].astype(F32)
              + gb_ref[...].astype(F32) * b_ref[...].astype(F32))
    y = jnp.dot(merged.astype(BF16), w_ref[...], preferred_element_type=F32)
    o_ref[...] = x_ref[...] + mod_ref[0, 2:3, :] * y


def _merge(proj, a, b, x, mod, w_out):
    tm = 512
    row = lambda i: (i, 0)
    return pl.pallas_call(
        _merge_kernel,
        grid=(N_TOK // tm,),
        in_specs=[pl.BlockSpec((tm, D_MODEL), lambda i: (i, COL_GA // D_MODEL)),
                  pl.BlockSpec((tm, D_MODEL), lambda i: (i, COL_GB // D_MODEL)),
                  pl.BlockSpec((tm, D_MODEL), row),
                  pl.BlockSpec((tm, D_MODEL), row),
                  pl.BlockSpec((tm, D_MODEL), row),
                  pl.BlockSpec((1, N_MOD, D_MODEL), lambda i: (i * tm // MOD_CHUNK, 0, 0)),
                  pl.BlockSpec((D_MODEL, D_MODEL), lambda i: (0, 0))],
        out_specs=pl.BlockSpec((tm, D_MODEL), row),
        out_shape=jax.ShapeDtypeStruct((N_TOK, D_MODEL), F32),
        compiler_params=_params(("arbitrary",)),
        name="merge_out",
    )(proj, proj, a, b, x, mod, w_out)


def _ffn_kernel(x_ref, mod_ref, n2_ref, wg_ref, wu_ref, wd_ref, o_ref, h_scr):
    f = pl.program_id(1)

    @pl.when(f == 0)
    def _():
        h = _modulated_norm(x_ref[...], n2_ref[...], mod_ref[0, 3:4, :], mod_ref[0, 4:5, :])
        h_scr[...] = h.astype(BF16)
        o_ref[...] = jnp.zeros_like(o_ref)

    h = h_scr[...]
    g = jnp.dot(h, wg_ref[...], preferred_element_type=F32)
    u = jnp.dot(h, wu_ref[...], preferred_element_type=F32)
    act = (g * jax.nn.sigmoid(g) * u).astype(BF16)
    o_ref[...] += jnp.dot(act, wd_ref[...], preferred_element_type=F32)

    @pl.when(f == pl.num_programs(1) - 1)
    def _():
        o_ref[...] = x_ref[...] + mod_ref[0, 5:6, :] * o_ref[...]


def _ffn(x, mod, norm2, wg, wu, wd):
    tm, tf = 1024, 256
    return pl.pallas_call(
        _ffn_kernel,
        grid=(N_TOK // tm, D_FF_DENSE // tf),
        in_specs=[pl.BlockSpec((tm, D_MODEL), lambda i, f: (i, 0)),
                  pl.BlockSpec((1, N_MOD, D_MODEL), lambda i, f: (i * tm // MOD_CHUNK, 0, 0)),
                  pl.BlockSpec((1, D_MODEL), lambda i, f: (0, 0)),
                  pl.BlockSpec((D_MODEL, tf), lambda i, f: (0, f)),
                  pl.BlockSpec((D_MODEL, tf), lambda i, f: (0, f)),
                  pl.BlockSpec((tf, D_MODEL), lambda i, f: (f, 0))],
        out_specs=pl.BlockSpec((tm, D_MODEL), lambda i, f: (i, 0)),
        out_shape=jax.ShapeDtypeStruct((N_TOK, D_MODEL), F32),
        scratch_shapes=[pltpu.VMEM((tm, D_MODEL), BF16)],
        compiler_params=_params(("arbitrary", "arbitrary")),
        name="ffn_dense",
    )(x, mod, norm2, wg, wu, wd)


def _moe_kernel(x_ref, mod_ref, n2_ref, rw_ref, wg_ref, wu_ref, wd_ref, o_ref, h_scr, gate_scr, acc_scr):
    e = pl.program_id(1)
    f = pl.program_id(2)
    last_f = pl.num_programs(2) - 1

    @pl.when((e == 0) & (f == 0))
    def _():
        h = _modulated_norm(x_ref[...], n2_ref[...], mod_ref[0, 3:4, :], mod_ref[0, 4:5, :])
        h_scr[...] = h.astype(BF16)
        logits = jnp.dot(h, rw_ref[...], preferred_element_type=F32, precision=lax.Precision.HIGHEST)
        lane = lax.broadcasted_iota(jnp.int32, logits.shape, 1).astype(F32)
        neg = jnp.float32(-jnp.inf)
        lg = jnp.where(lane < N_EXPERTS, logits, neg)
        m1 = jnp.max(lg, axis=-1, keepdims=True)
        i1 = jnp.min(jnp.where(lg == m1, lane, float(LANES)), axis=-1, keepdims=True)
        lg2 = jnp.where(lane == i1, neg, lg)
        m2 = jnp.max(lg2, axis=-1, keepdims=True)
        i2 = jnp.min(jnp.where(lg2 == m2, lane, float(LANES)), axis=-1, keepdims=True)
        e2 = jnp.exp(m2 - m1)
        den = 1.0 + e2
        gate_scr[...] = jnp.where(lane == i1, 1.0 / den, 0.0) + jnp.where(lane == i2, e2 / den, 0.0)
        o_ref[...] = jnp.zeros_like(o_ref)

    h = h_scr[...]
    g = jnp.dot(h, wg_ref[0], preferred_element_type=F32)
    u = jnp.dot(h, wu_ref[0], preferred_element_type=F32)
    act = (g * jax.nn.sigmoid(g) * u).astype(BF16)
    y = jnp.dot(act, wd_ref[0], preferred_element_type=F32)

    @pl.when(f == 0)
    def _():
        acc_scr[...] = y

    @pl.when(f > 0)
    def _():
        acc_scr[...] += y

    @pl.when(f == last_f)
    def _():
        lane = lax.broadcasted_iota(jnp.int32, gate_scr.shape, 1)
        gate = jnp.sum(jnp.where(lane == e, gate_scr[...], 0.0), axis=-1, keepdims=True)
        o_ref[...] += gate * acc_scr[...]

    @pl.when((e == pl.num_programs(1) - 1) & (f == last_f))
    def _():
        o_ref[...] = x_ref[...] + mod_ref[0, 5:6, :] * o_ref[...]


def _moe(x, mod, norm2, rw, wg, wu, wd):
    tm, tf = 1024, 512
    return pl.pallas_call(
        _moe_kernel,
        grid=(N_TOK // tm, N_EXPERTS, D_FF_EXPERT // tf),
        in_specs=[pl.BlockSpec((tm, D_MODEL), lambda i, e, f: (i, 0)),
                  pl.BlockSpec((1, N_MOD, D_MODEL), lambda i, e, f: (i * tm // MOD_CHUNK, 0, 0)),
                  pl.BlockSpec((1, D_MODEL), lambda i, e, f: (0, 0)),
                  pl.BlockSpec((D_MODEL, LANES), lambda i, e, f: (0, 0)),
                  pl.BlockSpec((1, D_MODEL, tf), lambda i, e, f: (e, 0, f)),
                  pl.BlockSpec((1, D_MODEL, tf), lambda i, e, f: (e, 0, f)),
                  pl.BlockSpec((1, tf, D_MODEL), lambda i, e, f: (e, f, 0))],
        out_specs=pl.BlockSpec((tm, D_MODEL), lambda i, e, f: (i, 0)),
        out_shape=jax.ShapeDtypeStruct((N_TOK, D_MODEL), F32),
        scratch_shapes=[pltpu.VMEM((tm, D_MODEL), BF16),
                        pltpu.VMEM((tm, LANES), F32),
                        pltpu.VMEM((tm, D_MODEL), F32)],
        compiler_params=_params(("arbitrary", "arbitrary", "arbitrary")),
        name="moe_experts",
    )(x, mod, norm2, rw, wg, wu, wd)


def _final_kernel(x_ref, g_ref, o_ref):
    x = x_ref[...]
    ms = jnp.mean(x * x, axis=-1, keepdims=True)
    o_ref[...] = x * lax.rsqrt(ms + NORM_EPS) * g_ref[...]


def _final_norm(x, gain):
    tm = 1024
    return pl.pallas_call(
        _final_kernel,
        grid=(N_TOK // tm,),
        in_specs=[pl.BlockSpec((tm, D_MODEL), lambda i: (i, 0)),
                  pl.BlockSpec((1, D_MODEL), lambda i: (0, 0))],
        out_specs=pl.BlockSpec((tm, D_MODEL), lambda i: (i, 0)),
        out_shape=jax.ShapeDtypeStruct((N_TOK, D_MODEL), F32),
        compiler_params=_params(("arbitrary",)),
        name="final_norm",
    )(x, gain)


def _rope_tables():
    pos = jnp.arange(DEC_SEQ, dtype=jnp.int32)
    d = jnp.arange(LANES, dtype=jnp.int32) % DIFF_HEAD_DIM

    def angles(p, theta, idx, half):
        inv_freq = jnp.exp(-math.log(theta) * jnp.arange(half, dtype=F32) / half)
        return p.astype(F32)[:, None] * inv_freq[idx][None, :]

    half_d = DIFF_ROT_DIM // 2
    ang = angles(pos, ROPE_THETA, d % half_d, half_d)
    active = (d < DIFF_ROT_DIM)[None, :]
    first = (d < half_d)[None, :]
    tab_d = jnp.stack([jnp.where(active, jnp.cos(ang), 1.0),
                       jnp.where(first, -jnp.sin(ang), 0.0),
                       jnp.where(active & ~first, jnp.sin(ang), 0.0)])

    half_g = GQA_HEAD_DIM // 4
    use_row = (d < GQA_HEAD_DIM // 2)[None, :]
    ang_r = angles(pos // GRID_W, AXIAL_THETA, d % half_g, half_g)
    ang_c = angles(pos % GRID_W, AXIAL_THETA, d % half_g, half_g)
    ang = jnp.where(use_row, ang_r, ang_c)
    first = ((d % (2 * half_g)) < half_g)[None, :]
    tab_g = jnp.stack([jnp.cos(ang),
                       jnp.where(first, -jnp.sin(ang), 0.0),
                       jnp.where(first, 0.0, jnp.sin(ang))])
    return tab_d.astype(F32), tab_g.astype(F32)


def _extend_w_in(w):
    def dup(cols):
        c = cols.reshape(D_MODEL, GQA_KV_HEADS, 1, GQA_HEAD_DIM)
        return jnp.broadcast_to(c, (D_MODEL, GQA_KV_HEADS, 2, GQA_HEAD_DIM)).reshape(D_MODEL, -1)
    kv = GQA_KV_HEADS * GQA_HEAD_DIM
    gk = w[:, COL_GK:COL_GK + kv]
    gv = w[:, COL_GK + kv:COL_GK + 2 * kv]
    return jnp.concatenate([w[:, :COL_GK], dup(gk), dup(gv), w[:, COL_GK + 2 * kv:]], axis=1)


def kernel(x_prompt, x_sample, c_prompt, c_sample, w_ada, b_ada, norm1, w_in, diff_lambda, diff_subln,
           gqa_q_norm, gqa_k_norm, w_out, norm2, ffn_w_gate, ffn_w_up, ffn_w_down, router_w,
           moe_w_gate, moe_w_up, moe_w_down, final_norm):
    x = jnp.concatenate([x_prompt.reshape(N_PROMPT, D_MODEL), x_sample.reshape(N_SAMPLE, D_MODEL)], axis=0)
    n_c = BATCH + DEC_BATCH
    c_all = jnp.concatenate([c_prompt, c_sample, jnp.zeros((16 - n_c, D_MODEL), F32)], axis=0)
    mod_all = _ada(c_all, w_ada, b_ada)
    chunk_src = jnp.asarray(list(range(BATCH)) + [BATCH + i // 2 for i in range(2 * DEC_BATCH)], jnp.int32)
    tab_d, tab_g = _rope_tables()
    grp = jnp.kron(jnp.eye(PROJ_TN // GQA_HEAD_DIM, dtype=F32),
                   jnp.full((GQA_HEAD_DIM, GQA_HEAD_DIM), 1.0 / GQA_HEAD_DIM, F32)).astype(BF16)

    for l in range(DEPTH):
        mod = mod_all[l][chunk_src].reshape(N_CHUNKS, N_MOD, D_MODEL)
        lam_init = 0.8 - 0.6 * math.exp(-0.3 * l)
        w_ext = _extend_w_in(w_in[l]).astype(BF16)
        gains = jnp.stack([jnp.tile(gqa_q_norm[l], PROJ_TN // GQA_HEAD_DIM),
                           jnp.tile(gqa_k_norm[l], PROJ_TN // GQA_HEAD_DIM)])
        proj = _proj(x, mod, norm1[l][None, :], w_ext, tab_d, tab_g, gains, grp)
        subln = diff_subln[l][None, :]
        parts_a, parts_b = [], []
        for batch, seq, row0 in ((BATCH, SEQ, 0), (DEC_BATCH, DEC_SEQ, N_PROMPT)):
            parts_a.append(_attention(proj, diff=True, batch=batch, seq=seq, row0=row0, lam_init=lam_init,
                                      lam=diff_lambda[l], subln=subln))
            parts_b.append(_attention(proj, diff=False, batch=batch, seq=seq, row0=row0))
        a = jnp.concatenate(parts_a, axis=0)
        b = jnp.concatenate(parts_b, axis=0)
        x = _merge(proj, a, b, x, mod, w_out[l].astype(BF16))
        if l % 2 == 0:
            x = _ffn(x, mod, norm2[l][None, :], ffn_w_gate[l // 2].astype(BF16),
                     ffn_w_up[l // 2].astype(BF16), ffn_w_down[l // 2].astype(BF16))
        else:
            rw = jnp.pad(router_w[l // 2], ((0, 0), (0, LANES - N_EXPERTS)))
            x = _moe(x, mod, norm2[l][None, :], rw, moe_w_gate[l // 2].astype(BF16),
                     moe_w_up[l // 2].astype(BF16), moe_w_down[l // 2].astype(BF16))
    y = _final_norm(x, final_norm[None, :])
    return (y[:N_PROMPT].reshape(BATCH, SEQ, D_MODEL), y[N_PROMPT:].reshape(DEC_BATCH, DEC_SEQ, D_MODEL))
```

```python
import functools
import math

import jax
import jax.numpy as jnp
from jax import lax
from jax.experimental import pallas as pl
from jax.experimental.pallas import tpu as pltpu

F32 = jnp.float32
BF16 = jnp.bfloat16

D_MODEL = 1024
BATCH = 8
SEQ = 4096
DEPTH = 4
DEC_BATCH = 2
DEC_SEQ = 8192
GRID_W = 64
DIFF_HEADS = 8
DIFF_HEAD_DIM = 64
DIFF_ROT_DIM = DIFF_HEAD_DIM // 4
ROPE_THETA = 500000.0
GQA_Q_HEADS = 16
GQA_KV_HEADS = 4
GQA_HEAD_DIM = 64
AXIAL_THETA = 10000.0
D_FF_DENSE = 2816
N_EXPERTS = 8
D_FF_EXPERT = 3584
NORM_EPS = 1e-6
N_MOD = 6

N_PROMPT = BATCH * SEQ
N_SAMPLE = DEC_BATCH * DEC_SEQ
N_TOK = N_PROMPT + N_SAMPLE
MOD_CHUNK = SEQ
N_CHUNKS = N_TOK // MOD_CHUNK

LANES = 128
SUBLANES = 8
MXU_COLS = 256
VMEM_LIMIT = 56 * 1024 * 1024

Q_SCALE = DIFF_HEAD_DIM ** -0.5 * math.log2(math.e)

PROJ_TN = 512
COL_DQ, COL_DK, COL_DV, COL_GQ = 0, 1024, 2048, 3072
COL_GK, COL_GV, COL_GA, COL_GB = 4096, 4608, 5120, 6144
PROJ_W = 7168
N_PROJ_TILES = PROJ_W // PROJ_TN

NT_DIMS = (((1,), (1,)), ((), ()))


def _params(sem, vmem=VMEM_LIMIT):
    return pltpu.CompilerParams(dimension_semantics=sem, vmem_limit_bytes=vmem)


def _sigmoid(x):
    return 0.5 * jnp.tanh(0.5 * x) + 0.5


def _ada_kernel(c_ref, w_ref, b_ref, o_ref):
    c = c_ref[...]
    ca = c * _sigmoid(c)
    o_ref[0] = jnp.dot(ca.astype(BF16), w_ref[0].astype(BF16), preferred_element_type=F32) + b_ref[0]


def _ada(c_all, w_ada, b_ada):
    rows = c_all.shape[0]
    tn = 1536
    width = N_MOD * D_MODEL
    return pl.pallas_call(
        _ada_kernel,
        grid=(DEPTH, width // tn),
        in_specs=[pl.BlockSpec((rows, D_MODEL), lambda l, j: (0, 0)),
                  pl.BlockSpec((1, D_MODEL, tn), lambda l, j: (l, 0, j)),
                  pl.BlockSpec((1, 1, tn), lambda l, j: (l, 0, j))],
        out_specs=pl.BlockSpec((1, rows, tn), lambda l, j: (l, 0, j)),
        out_shape=jax.ShapeDtypeStruct((DEPTH, rows, width), F32),
        compiler_params=_params(("arbitrary", "arbitrary")),
        name="ada_mod",
    )(c_all, w_ada, b_ada.reshape(DEPTH, 1, width))


def _modulated_norm(x, gain, shift, scale):
    ms = jnp.mean(x * x, axis=-1, keepdims=True)
    y = x * lax.rsqrt(ms + NORM_EPS) * gain
    return y * (1.0 + scale) + shift


def _rotate(x, c, sm, sp, shift):
    return x * c + pltpu.roll(x, LANES - shift, axis=1) * sm + pltpu.roll(x, shift, axis=1) * sp


def _pos_block(i, tm):
    return jnp.where(i < N_PROMPT // tm, i % (SEQ // tm), i % (DEC_SEQ // tm))


def _proj_kernel(x_ref, mod_ref, n1_ref, w_ref, tabd_ref, tabg_ref, gain_ref, grp_ref, o_ref, h_scr):
    j = pl.program_id(1)

    @pl.when(j == 0)
    def _():
        h = _modulated_norm(x_ref[...], n1_ref[...], mod_ref[0, 0:1, :], mod_ref[0, 1:2, :])
        h_scr[...] = h.astype(BF16)

    acc = jnp.dot(h_scr[...], w_ref[...], preferred_element_type=F32)
    n_slab = PROJ_TN // LANES
    q_tiles = DIFF_HEADS * 2 * DIFF_HEAD_DIM // PROJ_TN

    @pl.when(j < COL_DV // PROJ_TN)
    def _():
        c, sm, sp = tabd_ref[0], tabd_ref[1], tabd_ref[2]
        scale = jnp.where(j < q_tiles, Q_SCALE, 1.0)
        for s in range(n_slab):
            xs = acc[:, s * LANES:(s + 1) * LANES] * scale
            o_ref[:, s * LANES:(s + 1) * LANES] = _rotate(xs, c, sm, sp, DIFF_ROT_DIM // 2).astype(BF16)

    is_plain = ((j >= COL_DV // PROJ_TN) & (j < COL_GQ // PROJ_TN)) | (j == COL_GV // PROJ_TN)

    @pl.when(is_plain)
    def _():
        o_ref[...] = acc.astype(BF16)

    @pl.when((j >= COL_GQ // PROJ_TN) & (j < COL_GV // PROJ_TN))
    def _():
        is_q = j < COL_GK // PROJ_TN
        gain = gain_ref[pl.ds(jnp.where(is_q, 0, 1), 1), :] * jnp.where(is_q, Q_SCALE, 1.0)
        c, sm, sp = tabg_ref[0], tabg_ref[1], tabg_ref[2]
        for t in range(PROJ_TN // MXU_COLS):
            a = acc[:, t * MXU_COLS:(t + 1) * MXU_COLS]
            sq = a * a
            hi = sq.astype(BF16)
            lo = (sq - hi.astype(F32)).astype(BF16)
            ms = (jnp.dot(hi, grp_ref[...], preferred_element_type=F32)
                  + jnp.dot(lo, grp_ref[...], preferred_element_type=F32))
            y = a * lax.rsqrt(ms + NORM_EPS) * gain[:, t * MXU_COLS:(t + 1) * MXU_COLS]
            for s in range(MXU_COLS // LANES):
                col = t * MXU_COLS + s * LANES
                ys = y[:, s * LANES:(s + 1) * LANES]
                o_ref[:, col:col + LANES] = _rotate(ys, c, sm, sp, GQA_HEAD_DIM // 4).astype(BF16)

    @pl.when(j >= COL_GA // PROJ_TN)
    def _():
        o_ref[...] = _sigmoid(acc).astype(BF16)


def _proj(x, mod, norm1, w_ext, tab_d, tab_g, gains, grp):
    tm = 1024
    return pl.pallas_call(
        _proj_kernel,
        grid=(N_TOK // tm, N_PROJ_TILES),
        in_specs=[pl.BlockSpec((tm, D_MODEL), lambda i, j: (i, 0)),
                  pl.BlockSpec((1, N_MOD, D_MODEL), lambda i, j: (i * tm // MOD_CHUNK, 0, 0)),
                  pl.BlockSpec((1, D_MODEL), lambda i, j: (0, 0)),
                  pl.BlockSpec((D_MODEL, PROJ_TN), lambda i, j: (0, j)),
                  pl.BlockSpec((3, tm, LANES), lambda i, j: (0, _pos_block(i, tm), 0)),
                  pl.BlockSpec((3, tm, LANES), lambda i, j: (0, _pos_block(i, tm), 0)),
                  pl.BlockSpec((2, PROJ_TN), lambda i, j: (0, 0)),
                  pl.BlockSpec((MXU_COLS, MXU_COLS), lambda i, j: (0, 0))],
        out_specs=pl.BlockSpec((tm, PROJ_TN), lambda i, j: (i, j)),
        out_shape=jax.ShapeDtypeStruct((N_TOK, PROJ_W), BF16),
        scratch_shapes=[pltpu.VMEM((tm, D_MODEL), BF16)],
        compiler_params=_params(("arbitrary", "arbitrary")),
        name="proj_in",
    )(x, mod, norm1, w_ext, tab_d, tab_g, gains, grp)


ATT_TQ = 256
SUM_GROUP = 4


def _attn_kernel(*refs, diff, lam_init):
    if diff:
        q_ref, k_ref, v_ref, lam_ref, subln_ref, o_ref = refs
    else:
        q_ref, k_ref, v_ref, o_ref, vx_ref = refs
    half = LANES // 2
    seq = k_ref.shape[0]

    if not diff:
        @pl.when(pl.program_id(2) == 0)
        def _():
            v = v_ref[...]
            lane = lax.broadcasted_iota(jnp.int32, v.shape, 1)
            vx_ref[...] = jnp.where(lane < half, v, jnp.ones_like(v))

    q = q_ref[...]
    low = lax.broadcasted_iota(jnp.int32, q.shape, 1) < half
    zero = jnp.zeros_like(q)
    k = k_ref[...]

    def softmax_pv(qh):
        s = lax.dot_general(qh, k, NT_DIMS, preferred_element_type=F32)
        m = jnp.max(s, axis=-1, keepdims=True)
        if not diff:
            p = jnp.exp2((s - m).astype(BF16))
            return jnp.dot(p, vx_ref[...], preferred_element_type=F32)
        p = jnp.exp2(s - m)
        l = jnp.sum(p, axis=-1, keepdims=True)
        return jnp.dot(p.astype(BF16), v_ref[...], preferred_element_type=F32) / l

    acc_lo = softmax_pv(jnp.where(low, q, zero))
    acc_hi = softmax_pv(jnp.where(low, zero, q))
    if diff:
        o_lo, o_hi = acc_lo, acc_hi
        lp = lam_ref[...]
        t1 = jnp.sum(lp[0:1] * lp[1:2], axis=-1, keepdims=True)
        t2 = jnp.sum(lp[2:3] * lp[3:4], axis=-1, keepdims=True)
        lam = jnp.exp(t1) - jnp.exp(t2) + lam_init
        a = o_lo - lam * o_hi
        ms = jnp.mean(a * a, axis=-1, keepdims=True)
        a = a * lax.rsqrt(ms + NORM_EPS) * subln_ref[...] * (1.0 - lam_init)
        o_ref[...] = a.astype(BF16)
    else:
        out = jnp.where(low, acc_lo / pltpu.roll(acc_lo, half, axis=1), pltpu.roll(acc_hi, half, axis=1) / acc_hi)
        o_ref[...] = out.astype(BF16)


def _attention(proj, *, diff, batch, seq, row0, lam_init=0.0, lam=None, subln=None):
    tq = ATT_TQ
    nq = seq // tq
    q0 = row0 // tq
    s0 = row0 // seq
    n_units = DIFF_HEADS
    if diff:
        qc, kc, vc = COL_DQ // LANES, COL_DK // LANES, COL_DV // LANES
        kmap = lambda b, h, i: (s0 + b, kc + h)
        vmap = lambda b, h, i: (s0 + b, vc + h)
    else:
        qc, kc, vc = COL_GQ // LANES, COL_GK // LANES, COL_GV // LANES
        kmap = lambda b, h, i: (s0 + b, kc + h // 2)
        vmap = lambda b, h, i: (s0 + b, vc + h // 2)
    in_specs = [pl.BlockSpec((tq, LANES), lambda b, h, i: (q0 + b * nq + i, qc + h)),
                pl.BlockSpec((seq, LANES), kmap),
                pl.BlockSpec((seq, LANES), vmap)]
    args = [proj, proj, proj]
    if diff:
        in_specs += [pl.BlockSpec((4, DIFF_HEAD_DIM), lambda b, h, i: (0, 0)),
                     pl.BlockSpec((1, LANES), lambda b, h, i: (0, 0))]
        args += [lam, subln]
    return pl.pallas_call(
        functools.partial(_attn_kernel, diff=diff, lam_init=lam_init),
        grid=(batch, n_units, nq),
        in_specs=in_specs,
        out_specs=pl.BlockSpec((tq, LANES), lambda b, h, i: (b * nq + i, h)),
        out_shape=jax.ShapeDtypeStruct((batch * seq, D_MODEL), BF16),
        scratch_shapes=[] if diff else [pltpu.VMEM((seq, LANES), BF16)],
        compiler_params=_params(("arbitrary", "arbitrary", "arbitrary")),
        name=("diff_attn" if diff else "gqa_attn") + f"_{seq}",
    )(*args)


def _merge_kernel(ga_ref, gb_ref, a_ref, b_ref, x_ref, mod_ref, w_ref, o_ref):
    merged = (ga_ref[...].astype(F32) * a_ref[...].astype(F32)
              + gb_ref[...].astype(F32) * b_ref[...].astype(F32))
    y = jnp.dot(merged.astype(BF16), w_ref[...], preferred_element_type=F32)
    o_ref[...] = x_ref[...] + mod_ref[0, 2:3, :] * y


def _merge(proj, a, b, x, mod, w_out):
    tm = 512
    row = lambda i: (i, 0)
    return pl.pallas_call(
        _merge_kernel,
        grid=(N_TOK // tm,),
        in_specs=[pl.BlockSpec((tm, D_MODEL), lambda i: (i, COL_GA // D_MODEL)),
                  pl.BlockSpec((tm, D_MODEL), lambda i: (i, COL_GB // D_MODEL)),
                  pl.BlockSpec((tm, D_MODEL), row),
                  pl.BlockSpec((tm, D_MODEL), row),
                  pl.BlockSpec((tm, D_MODEL), row),
                  pl.BlockSpec((1, N_MOD, D_MODEL), lambda i: (i * tm // MOD_CHUNK, 0, 0)),
                  pl.BlockSpec((D_MODEL, D_MODEL), lambda i: (0, 0))],
        out_specs=pl.BlockSpec((tm, D_MODEL), row),
        out_shape=jax.ShapeDtypeStruct((N_TOK, D_MODEL), F32),
        compiler_params=_params(("arbitrary",)),
        name="merge_out",
    )(proj, proj, a, b, x, mod, w_out)


def _ffn_kernel(x_ref, mod_ref, n2_ref, wg_ref, wu_ref, wd_ref, o_ref, h_scr):
    f = pl.program_id(1)

    @pl.when(f == 0)
    def _():
        h = _modulated_norm(x_ref[...], n2_ref[...], mod_ref[0, 3:4, :], mod_ref[0, 4:5, :])
        h_scr[...] = h.astype(BF16)
        o_ref[...] = jnp.zeros_like(o_ref)

    h = h_scr[...]
    g = jnp.dot(h, wg_ref[...], preferred_element_type=F32)
    u = jnp.dot(h, wu_ref[...], preferred_element_type=F32)
    act = (g * _sigmoid(g) * u).astype(BF16)
    o_ref[...] += jnp.dot(act, wd_ref[...], preferred_element_type=F32)

    @pl.when(f == pl.num_programs(1) - 1)
    def _():
        o_ref[...] = x_ref[...] + mod_ref[0, 5:6, :] * o_ref[...]


def _ffn(x, mod, norm2, wg, wu, wd):
    tm, tf = 1024, 256
    return pl.pallas_call(
        _ffn_kernel,
        grid=(N_TOK // tm, D_FF_DENSE // tf),
        in_specs=[pl.BlockSpec((tm, D_MODEL), lambda i, f: (i, 0)),
                  pl.BlockSpec((1, N_MOD, D_MODEL), lambda i, f: (i * tm // MOD_CHUNK, 0, 0)),
                  pl.BlockSpec((1, D_MODEL), lambda i, f: (0, 0)),
                  pl.BlockSpec((D_MODEL, tf), lambda i, f: (0, f)),
                  pl.BlockSpec((D_MODEL, tf), lambda i, f: (0, f)),
                  pl.BlockSpec((tf, D_MODEL), lambda i, f: (f, 0))],
        out_specs=pl.BlockSpec((tm, D_MODEL), lambda i, f: (i, 0)),
        out_shape=jax.ShapeDtypeStruct((N_TOK, D_MODEL), F32),
        scratch_shapes=[pltpu.VMEM((tm, D_MODEL), BF16)],
        compiler_params=_params(("arbitrary", "arbitrary")),
        name="ffn_dense",
    )(x, mod, norm2, wg, wu, wd)


MOE_TM = 1024


def _router_kernel(x_ref, mod_ref, n2_ref, rw_ref, ht_ref, gate_ref, rank_ref, rankt_ref, cnt_ref):
    h = _modulated_norm(x_ref[...], n2_ref[...], mod_ref[0, 3:4, :], mod_ref[0, 4:5, :])
    ht_ref[...] = h.T.astype(BF16)
    logits = jnp.dot(h, rw_ref[...], preferred_element_type=F32, precision=lax.Precision.HIGHEST)
    tm = h.shape[0]
    lane = lax.broadcasted_iota(jnp.int32, logits.shape, 1).astype(F32)
    neg = jnp.float32(-jnp.inf)
    lg = jnp.where(lane < N_EXPERTS, logits, neg)
    m1 = jnp.max(lg, axis=-1, keepdims=True)
    i1 = jnp.min(jnp.where(lg == m1, lane, float(LANES)), axis=-1, keepdims=True)
    lg2 = jnp.where(lane == i1, neg, lg)
    m2 = jnp.max(lg2, axis=-1, keepdims=True)
    i2 = jnp.min(jnp.where(lg2 == m2, lane, float(LANES)), axis=-1, keepdims=True)
    e2 = jnp.exp(m2 - m1)
    den = 1.0 + e2
    gate_ref[...] = jnp.where(lane == i1, 1.0 / den, 0.0) + jnp.where(lane == i2, e2 / den, 0.0)
    sel = (lane == i1) | (lane == i2)
    sel_f = jnp.where(sel, 1.0, 0.0)
    before = (lax.broadcasted_iota(jnp.int32, (tm, tm), 1)
              < lax.broadcasted_iota(jnp.int32, (tm, tm), 0))
    slot = jnp.dot(jnp.where(before, 1.0, 0.0).astype(BF16), sel_f.astype(BF16), preferred_element_type=F32)
    rank = jnp.where(sel, slot, -1.0)
    rank_ref[...] = rank
    rankt_ref[...] = rank.T[:SUBLANES, :]
    cnt = jnp.sum(sel_f, axis=0, keepdims=True)
    cnt_ref[...] = jnp.broadcast_to(cnt, (SUBLANES, LANES)).astype(jnp.int32)


def _router(x, mod, norm2, rw, *, tm=MOE_TM, mod_chunk=MOD_CHUNK):
    n, d = x.shape
    nt = n // tm
    return pl.pallas_call(
        _router_kernel,
        grid=(nt,),
        in_specs=[pl.BlockSpec((tm, d), lambda i: (i, 0)),
                  pl.BlockSpec((1, N_MOD, d), lambda i: (i * tm // mod_chunk, 0, 0)),
                  pl.BlockSpec((1, d), lambda i: (0, 0)),
                  pl.BlockSpec((d, LANES), lambda i: (0, 0))],
        out_specs=[pl.BlockSpec((d, tm), lambda i: (0, i)),
                   pl.BlockSpec((tm, LANES), lambda i: (i, 0)),
                   pl.BlockSpec((tm, LANES), lambda i: (i, 0)),
                   pl.BlockSpec((SUBLANES, tm), lambda i: (i, 0)),
                   pl.BlockSpec((SUBLANES, LANES), lambda i: (i, 0))],
        out_shape=[jax.ShapeDtypeStruct((d, n), BF16),
                   jax.ShapeDtypeStruct((n, LANES), F32),
                   jax.ShapeDtypeStruct((n, LANES), F32),
                   jax.ShapeDtypeStruct((nt * SUBLANES, tm), F32),
                   jax.ShapeDtypeStruct((nt * SUBLANES, LANES), jnp.int32)],
        compiler_params=_params(("arbitrary",)),
        name="moe_router",
    )(x, mod, norm2, rw)


def _moe_kernel(cnt_ref, ht_ref, rankt_ref, rank_ref, gate_ref, x_ref, mod_ref, wg_ref, wu_ref, wd_ref,
                o_ref, xt_scr, yt_scr):
    i, e, f = pl.program_id(0), pl.program_id(1), pl.program_id(2)
    last_f = pl.num_programs(2) - 1
    tm = o_ref.shape[0]
    n_blk = (cnt_ref[i * N_EXPERTS + e] + MXU_COLS - 1) // MXU_COLS

    @pl.when((e == 0) & (f == 0))
    def _():
        o_ref[...] = jnp.zeros_like(o_ref)

    @pl.when(f == 0)
    def _():
        slot = rankt_ref[pl.ds(e, 1), :].astype(jnp.int32)
        sub = lax.broadcasted_iota(jnp.int32, (MXU_COLS, tm), 0)

        def gather(b, carry):
            pick = jnp.where(sub + b * MXU_COLS == slot, 1.0, 0.0).astype(BF16)
            xt_scr[b] = lax.dot_general(ht_ref[...], pick, NT_DIMS, preferred_element_type=F32).astype(BF16)
            yt_scr[b] = jnp.zeros(yt_scr.shape[1:], F32)
            return carry

        lax.fori_loop(0, n_blk, gather, 0)

    def expert(b, carry):
        xb = xt_scr[b]
        g = jnp.dot(wg_ref[0], xb, preferred_element_type=F32)
        u = jnp.dot(wu_ref[0], xb, preferred_element_type=F32)
        act = (g * _sigmoid(g) * u).astype(BF16)
        yt_scr[b] += jnp.dot(wd_ref[0], act, preferred_element_type=F32)
        return carry

    lax.fori_loop(0, n_blk, expert, 0)

    @pl.when(f == last_f)
    def _():
        on_lane = lax.broadcasted_iota(jnp.int32, (tm, LANES), 1) == e
        slot = jnp.sum(jnp.where(on_lane, rank_ref[...], 0.0), axis=-1, keepdims=True).astype(jnp.int32)
        gate = jnp.sum(jnp.where(on_lane, gate_ref[...], 0.0), axis=-1, keepdims=True)
        col = lax.broadcasted_iota(jnp.int32, (tm, MXU_COLS), 1)

        def scatter(b, carry):
            pick = jnp.where(col + b * MXU_COLS == slot, 1.0, 0.0).astype(BF16)
            y = lax.dot_general(pick, yt_scr[b].astype(BF16), NT_DIMS, preferred_element_type=F32)
            o_ref[...] += gate * y
            return carry

        lax.fori_loop(0, n_blk, scatter, 0)

    @pl.when((e == pl.num_programs(1) - 1) & (f == last_f))
    def _():
        o_ref[...] = x_ref[...] + mod_ref[0, 5:6, :] * o_ref[...]


def _moe(counts, ht, rankt, rank, gate, x, mod, wgt, wut, wdt, *, tm=MOE_TM, tf=896, mod_chunk=MOD_CHUNK):
    n, d = x.shape
    n_exp, d_ff, _ = wgt.shape
    grid_spec = pltpu.PrefetchScalarGridSpec(
        num_scalar_prefetch=1,
        grid=(n // tm, n_exp, d_ff // tf),
        in_specs=[pl.BlockSpec((d, tm), lambda i, e, f, c: (0, i)),
                  pl.BlockSpec((SUBLANES, tm), lambda i, e, f, c: (i, 0)),
                  pl.BlockSpec((tm, LANES), lambda i, e, f, c: (i, 0)),
                  pl.BlockSpec((tm, LANES), lambda i, e, f, c: (i, 0)),
                  pl.BlockSpec((tm, d), lambda i, e, f, c: (i, 0)),
                  pl.BlockSpec((1, N_MOD, d), lambda i, e, f, c: (i * tm // mod_chunk, 0, 0)),
                  pl.BlockSpec((1, tf, d), lambda i, e, f, c: (e, f, 0)),
                  pl.BlockSpec((1, tf, d), lambda i, e, f, c: (e, f, 0)),
                  pl.BlockSpec((1, d, tf), lambda i, e, f, c: (e, 0, f))],
        out_specs=pl.BlockSpec((tm, d), lambda i, e, f, c: (i, 0)),
        scratch_shapes=[pltpu.VMEM((tm // MXU_COLS, d, MXU_COLS), BF16),
                        pltpu.VMEM((tm // MXU_COLS, d, MXU_COLS), F32)])
    return pl.pallas_call(
        _moe_kernel,
        grid_spec=grid_spec,
        out_shape=jax.ShapeDtypeStruct((n, d), F32),
        compiler_params=_params(("arbitrary", "arbitrary", "arbitrary")),
        name="moe_experts",
    )(counts, ht, rankt, rank, gate, x, mod, wgt, wut, wdt)


def _moe_layer(x, mod, norm2, router_w, w_gate, w_up, w_down):
    rw = jnp.pad(router_w, ((0, 0), (0, LANES - N_EXPERTS)))
    ht, gate, rank, rankt, cnt = _router(x, mod, norm2, rw)
    counts = cnt.reshape(-1, SUBLANES, LANES)[:, 0, :N_EXPERTS].reshape(-1)
    wgt = jnp.swapaxes(w_gate, 1, 2).astype(BF16)
    wut = jnp.swapaxes(w_up, 1, 2).astype(BF16)
    wdt = jnp.swapaxes(w_down, 1, 2).astype(BF16)
    return _moe(counts, ht, rankt, rank, gate, x, mod, wgt, wut, wdt)


def _final_kernel(x_ref, g_ref, o_ref):
    x = x_ref[...]
    ms = jnp.mean(x * x, axis=-1, keepdims=True)
    o_ref[...] = x * lax.rsqrt(ms + NORM_EPS) * g_ref[...]


def _final_norm(x, gain):
    tm = 1024
    return pl.pallas_call(
        _final_kernel,
        grid=(N_TOK // tm,),
        in_specs=[pl.BlockSpec((tm, D_MODEL), lambda i: (i, 0)),
                  pl.BlockSpec((1, D_MODEL), lambda i: (0, 0))],
        out_specs=pl.BlockSpec((tm, D_MODEL), lambda i: (i, 0)),
        out_shape=jax.ShapeDtypeStruct((N_TOK, D_MODEL), F32),
        compiler_params=_params(("arbitrary",)),
        name="final_norm",
    )(x, gain)


def _rope_tables():
    pos = jnp.arange(DEC_SEQ, dtype=jnp.int32)
    d = jnp.arange(LANES, dtype=jnp.int32) % DIFF_HEAD_DIM

    def angles(p, theta, idx, half):
        inv_freq = jnp.exp(-math.log(theta) * jnp.arange(half, dtype=F32) / half)
        return p.astype(F32)[:, None] * inv_freq[idx][None, :]

    half_d = DIFF_ROT_DIM // 2
    ang = angles(pos, ROPE_THETA, d % half_d, half_d)
    active = (d < DIFF_ROT_DIM)[None, :]
    first = (d < half_d)[None, :]
    tab_d = jnp.stack([jnp.where(active, jnp.cos(ang), 1.0),
                       jnp.where(first, -jnp.sin(ang), 0.0),
                       jnp.where(active & ~first, jnp.sin(ang), 0.0)])

    half_g = GQA_HEAD_DIM // 4
    use_row = (d < GQA_HEAD_DIM // 2)[None, :]
    ang_r = angles(pos // GRID_W, AXIAL_THETA, d % half_g, half_g)
    ang_c = angles(pos % GRID_W, AXIAL_THETA, d % half_g, half_g)
    ang = jnp.where(use_row, ang_r, ang_c)
    first = ((d % (2 * half_g)) < half_g)[None, :]
    tab_g = jnp.stack([jnp.cos(ang),
                       jnp.where(first, -jnp.sin(ang), 0.0),
                       jnp.where(first, 0.0, jnp.sin(ang))])
    return tab_d.astype(F32), tab_g.astype(F32)


def _extend_w_in(w):
    def dup(cols):
        c = cols.reshape(D_MODEL, GQA_KV_HEADS, 1, GQA_HEAD_DIM)
        return jnp.broadcast_to(c, (D_MODEL, GQA_KV_HEADS, 2, GQA_HEAD_DIM)).reshape(D_MODEL, -1)
    kv = GQA_KV_HEADS * GQA_HEAD_DIM
    gk = w[:, COL_GK:COL_GK + kv]
    gv = w[:, COL_GK + kv:COL_GK + 2 * kv]
    return jnp.concatenate([w[:, :COL_GK], dup(gk), dup(gv), w[:, COL_GK + 2 * kv:]], axis=1)


def kernel(x_prompt, x_sample, c_prompt, c_sample, w_ada, b_ada, norm1, w_in, diff_lambda, diff_subln,
           gqa_q_norm, gqa_k_norm, w_out, norm2, ffn_w_gate, ffn_w_up, ffn_w_down, router_w,
           moe_w_gate, moe_w_up, moe_w_down, final_norm):
    x = jnp.concatenate([x_prompt.reshape(N_PROMPT, D_MODEL), x_sample.reshape(N_SAMPLE, D_MODEL)], axis=0)
    n_c = BATCH + DEC_BATCH
    c_all = jnp.concatenate([c_prompt, c_sample, jnp.zeros((16 - n_c, D_MODEL), F32)], axis=0)
    mod_all = _ada(c_all, w_ada, b_ada)
    chunk_src = jnp.asarray(list(range(BATCH)) + [BATCH + i // 2 for i in range(2 * DEC_BATCH)], jnp.int32)
    tab_d, tab_g = _rope_tables()
    grp = jnp.kron(jnp.eye(MXU_COLS // GQA_HEAD_DIM, dtype=F32),
                   jnp.full((GQA_HEAD_DIM, GQA_HEAD_DIM), 1.0 / GQA_HEAD_DIM, F32)).astype(BF16)

    for l in range(DEPTH):
        mod = mod_all[l][chunk_src].reshape(N_CHUNKS, N_MOD, D_MODEL)
        lam_init = 0.8 - 0.6 * math.exp(-0.3 * l)
        w_ext = _extend_w_in(w_in[l]).astype(BF16)
        gains = jnp.stack([jnp.tile(gqa_q_norm[l], PROJ_TN // GQA_HEAD_DIM),
                           jnp.tile(gqa_k_norm[l], PROJ_TN // GQA_HEAD_DIM)])
        proj = _proj(x, mod, norm1[l][None, :], w_ext, tab_d, tab_g, gains, grp)
        subln = diff_subln[l][None, :]
        parts_a, parts_b = [], []
        for batch, seq, row0 in ((BATCH, SEQ, 0), (DEC_BATCH, DEC_SEQ, N_PROMPT)):
            parts_a.append(_attention(proj, diff=True, batch=batch, seq=seq, row0=row0, lam_init=lam_init,
                                      lam=diff_lambda[l], subln=subln))
            parts_b.append(_attention(proj, diff=False, batch=batch, seq=seq, row0=row0))
        a = jnp.concatenate(parts_a, axis=0)
        b = jnp.concatenate(parts_b, axis=0)
        x = _merge(proj, a, b, x, mod, w_out[l].astype(BF16))
        if l % 2 == 0:
            x = _ffn(x, mod, norm2[l][None, :], ffn_w_gate[l // 2].astype(BF16),
                     ffn_w_up[l // 2].astype(BF16), ffn_w_down[l // 2].astype(BF16))
        else:
            x = _moe_layer(x, mod, norm2[l][None, :], router_w[l // 2], moe_w_gate[l // 2],
                           moe_w_up[l // 2], moe_w_down[l // 2])
    y = _final_norm(x, final_norm[None, :])
    return (y[:N_PROMPT].reshape(BATCH, SEQ, D_MODEL), y[N_PROMPT:].reshape(DEC_BATCH, DEC_SEQ, D_MODEL))
```

```python
import functools
import math

import jax
import jax.numpy as jnp
from jax import lax
from jax.experimental import pallas as pl
from jax.experimental.pallas import tpu as pltpu

F32 = jnp.float32
BF16 = jnp.bfloat16

D_MODEL = 1024
BATCH = 8
SEQ = 4096
DEPTH = 4
DEC_BATCH = 2
DEC_SEQ = 8192
GRID_W = 64
DIFF_HEADS = 8
DIFF_HEAD_DIM = 64
DIFF_ROT_DIM = DIFF_HEAD_DIM // 4
ROPE_THETA = 500000.0
GQA_Q_HEADS = 16
GQA_KV_HEADS = 4
GQA_HEAD_DIM = 64
AXIAL_THETA = 10000.0
D_FF_DENSE = 2816
N_EXPERTS = 8
D_FF_EXPERT = 3584
NORM_EPS = 1e-6
N_MOD = 6

N_PROMPT = BATCH * SEQ
N_SAMPLE = DEC_BATCH * DEC_SEQ
N_TOK = N_PROMPT + N_SAMPLE
MOD_CHUNK = SEQ
N_CHUNKS = N_TOK // MOD_CHUNK

LANES = 128
SUBLANES = 8
MXU_COLS = 256
VMEM_LIMIT = 56 * 1024 * 1024

Q_SCALE = DIFF_HEAD_DIM ** -0.5 * math.log2(math.e)

PROJ_TN = 512
COL_DQ, COL_DK, COL_DV, COL_GQ = 0, 1024, 2048, 3072
COL_GK, COL_GV, COL_GA, COL_GB = 4096, 4608, 5120, 6144
PROJ_W = 7168
N_PROJ_TILES = PROJ_W // PROJ_TN

NT_DIMS = (((1,), (1,)), ((), ()))


def _params(sem, vmem=VMEM_LIMIT):
    return pltpu.CompilerParams(dimension_semantics=sem, vmem_limit_bytes=vmem)


def _sigmoid(x):
    return 0.5 * jnp.tanh(0.5 * x) + 0.5


def _ada_kernel(c_ref, w_ref, b_ref, o_ref):
    c = c_ref[...]
    ca = c * _sigmoid(c)
    o_ref[0] = jnp.dot(ca.astype(BF16), w_ref[0].astype(BF16), preferred_element_type=F32) + b_ref[0]


def _ada(c_all, w_ada, b_ada):
    rows = c_all.shape[0]
    tn = 1536
    width = N_MOD * D_MODEL
    return pl.pallas_call(
        _ada_kernel,
        grid=(DEPTH, width // tn),
        in_specs=[pl.BlockSpec((rows, D_MODEL), lambda l, j: (0, 0)),
                  pl.BlockSpec((1, D_MODEL, tn), lambda l, j: (l, 0, j)),
                  pl.BlockSpec((1, 1, tn), lambda l, j: (l, 0, j))],
        out_specs=pl.BlockSpec((1, rows, tn), lambda l, j: (l, 0, j)),
        out_shape=jax.ShapeDtypeStruct((DEPTH, rows, width), F32),
        compiler_params=_params(("arbitrary", "arbitrary")),
        name="ada_mod",
    )(c_all, w_ada, b_ada.reshape(DEPTH, 1, width))


def _modulated_norm(x, gain, shift, scale):
    ms = jnp.mean(x * x, axis=-1, keepdims=True)
    y = x * lax.rsqrt(ms + NORM_EPS) * gain
    return y * (1.0 + scale) + shift


def _rotate(x, c, sm, sp, shift):
    return x * c + pltpu.roll(x, LANES - shift, axis=1) * sm + pltpu.roll(x, shift, axis=1) * sp


def _pos_block(i, tm):
    return jnp.where(i < N_PROMPT // tm, i % (SEQ // tm), i % (DEC_SEQ // tm))


def _proj_kernel(x_ref, mod_ref, n1_ref, w_ref, tabd_ref, tabg_ref, gain_ref, grp_ref, o_ref, h_scr, acc_a, acc_b):
    j = pl.program_id(1)
    tile = j - 1
    n_slab = PROJ_TN // LANES
    q_tiles = DIFF_HEADS * 2 * DIFF_HEAD_DIM // PROJ_TN

    def matmul(dst):
        dst[...] = jnp.dot(h_scr[...], w_ref[...], preferred_element_type=F32)

    def rope_epilogue(src):
        c, sm, sp = tabd_ref[0], tabd_ref[1], tabd_ref[2]
        scale = jnp.where(tile < q_tiles, Q_SCALE, 1.0)
        for s in range(n_slab):
            xs = src[:, s * LANES:(s + 1) * LANES] * scale
            o_ref[:, s * LANES:(s + 1) * LANES] = _rotate(xs, c, sm, sp, DIFF_ROT_DIM // 2).astype(BF16)

    def plain_epilogue(src):
        o_ref[...] = src[...].astype(BF16)

    def norm_rope_epilogue(src):
        is_q = tile < COL_GK // PROJ_TN
        gain = gain_ref[pl.ds(jnp.where(is_q, 0, 1), 1), :] * jnp.where(is_q, Q_SCALE, 1.0)
        c, sm, sp = tabg_ref[0], tabg_ref[1], tabg_ref[2]
        for t in range(PROJ_TN // MXU_COLS):
            a = src[:, t * MXU_COLS:(t + 1) * MXU_COLS]
            sq = a * a
            hi = sq.astype(BF16)
            lo = (sq - hi.astype(F32)).astype(BF16)
            ms = (jnp.dot(hi, grp_ref[...], preferred_element_type=F32)
                  + jnp.dot(lo, grp_ref[...], preferred_element_type=F32))
            y = a * lax.rsqrt(ms + NORM_EPS) * gain[:, t * MXU_COLS:(t + 1) * MXU_COLS]
            for s in range(MXU_COLS // LANES):
                col = t * MXU_COLS + s * LANES
                ys = y[:, s * LANES:(s + 1) * LANES]
                o_ref[:, col:col + LANES] = _rotate(ys, c, sm, sp, GQA_HEAD_DIM // 4).astype(BF16)

    def gate_epilogue(src):
        for s in range(n_slab):
            o_ref[:, s * LANES:(s + 1) * LANES] = _sigmoid(src[:, s * LANES:(s + 1) * LANES]).astype(BF16)

    is_plain = ((tile >= COL_DV // PROJ_TN) & (tile < COL_GQ // PROJ_TN)) | (tile == COL_GV // PROJ_TN)
    kinds = ((tile < COL_DV // PROJ_TN, rope_epilogue),
             (is_plain, plain_epilogue),
             ((tile >= COL_GQ // PROJ_TN) & (tile < COL_GV // PROJ_TN), norm_rope_epilogue),
             (tile >= COL_GA // PROJ_TN, gate_epilogue))

    @pl.when(j == 0)
    def _():
        h = _modulated_norm(x_ref[...], n1_ref[...], mod_ref[0, 0:1, :], mod_ref[0, 1:2, :])
        h_scr[...] = h.astype(BF16)
        matmul(acc_a)

    for parity, cur, prev in ((0, acc_a, acc_b), (1, acc_b, acc_a)):
        for cond, epilogue in kinds:
            @pl.when((j > 0) & (j < N_PROJ_TILES) & (j % 2 == parity) & cond)
            def _(cur=cur, prev=prev, epilogue=epilogue):
                matmul(cur)
                epilogue(prev)

    @pl.when(j == N_PROJ_TILES)
    def _():
        gate_epilogue(acc_a if N_PROJ_TILES % 2 else acc_b)


def _proj(x, mod, norm1, w_ext, tab_d, tab_g, gains, grp):
    tm = 1024
    last = N_PROJ_TILES - 1
    return pl.pallas_call(
        _proj_kernel,
        grid=(N_TOK // tm, N_PROJ_TILES + 1),
        in_specs=[pl.BlockSpec((tm, D_MODEL), lambda i, j: (i, 0)),
                  pl.BlockSpec((1, N_MOD, D_MODEL), lambda i, j: (i * tm // MOD_CHUNK, 0, 0)),
                  pl.BlockSpec((1, D_MODEL), lambda i, j: (0, 0)),
                  pl.BlockSpec((D_MODEL, PROJ_TN), lambda i, j: (0, jnp.minimum(j, last))),
                  pl.BlockSpec((3, tm, LANES), lambda i, j: (0, _pos_block(i, tm), 0)),
                  pl.BlockSpec((3, tm, LANES), lambda i, j: (0, _pos_block(i, tm), 0)),
                  pl.BlockSpec((2, PROJ_TN), lambda i, j: (0, 0)),
                  pl.BlockSpec((MXU_COLS, MXU_COLS), lambda i, j: (0, 0))],
        out_specs=pl.BlockSpec((tm, PROJ_TN), lambda i, j: (i, jnp.maximum(j - 1, 0))),
        out_shape=jax.ShapeDtypeStruct((N_TOK, PROJ_W), BF16),
        scratch_shapes=[pltpu.VMEM((tm, D_MODEL), BF16), pltpu.VMEM((tm, PROJ_TN), F32),
                        pltpu.VMEM((tm, PROJ_TN), F32)],
        compiler_params=_params(("arbitrary", "arbitrary")),
        name="proj_in",
    )(x, mod, norm1, w_ext, tab_d, tab_g, gains, grp)


ATT_SCORE_BYTES = 4 * 1024 * 1024


def _attn_kernel(*refs, diff, lam_init, nq):
    if diff:
        q_ref, k_ref, v_ref, lam_ref, subln_ref, o_ref = refs[:6]
        bufs, vx_ref = refs[6:-1], refs[-1]
    else:
        q_ref, k_ref, v_ref, o_ref = refs[:4]
        bufs, vx_ref = refs[4:-1], refs[-1]
    set_a, set_b = bufs[:4], bufs[4:]
    half = LANES // 2
    step = pl.program_id(0)

    @pl.when(step == 0)
    def _():
        for ref in set_b:
            ref[...] = jnp.zeros_like(ref)

    @pl.when(jnp.maximum(step - 1, 0) % nq == 0)
    def _():
        v = v_ref[...]
        if diff:
            vx_ref[:, :LANES] = v
            vx_ref[:, LANES:] = jnp.ones_like(v)
        else:
            lane = lax.broadcasted_iota(jnp.int32, v.shape, 1)
            vx_ref[...] = jnp.where(lane < half, v, jnp.ones_like(v))

    def finish(s_ref, m_ref):
        p = jnp.exp2((s_ref[...] - m_ref[...]).astype(BF16))
        acc = jnp.dot(p, vx_ref[...], preferred_element_type=F32)
        return acc[:, :LANES] / acc[:, LANES:] if diff else acc

    def start(qh, s_ref, m_ref):
        s = lax.dot_general(qh, k_ref[...], NT_DIMS, preferred_element_type=F32)
        s_ref[...] = s
        m_ref[...] = jnp.max(s, axis=-1, keepdims=True)

    def run(cur, prev):
        q = q_ref[...]
        low = lax.broadcasted_iota(jnp.int32, q.shape, 1) < half
        zero = jnp.zeros_like(q)
        acc_lo = finish(prev[0], prev[2])
        start(jnp.where(low, q, zero), cur[0], cur[2])
        acc_hi = finish(prev[1], prev[3])
        start(jnp.where(low, zero, q), cur[1], cur[3])
        if diff:
            lp = lam_ref[...]
            t1 = jnp.sum(lp[0:1] * lp[1:2], axis=-1, keepdims=True)
            t2 = jnp.sum(lp[2:3] * lp[3:4], axis=-1, keepdims=True)
            lam = jnp.exp(t1) - jnp.exp(t2) + lam_init
            a = acc_lo - lam * acc_hi
            ms = jnp.mean(a * a, axis=-1, keepdims=True)
            a = a * lax.rsqrt(ms + NORM_EPS) * subln_ref[...] * (1.0 - lam_init)
            o_ref[...] = a.astype(BF16)
        else:
            out = jnp.where(low, acc_lo / pltpu.roll(acc_lo, half, axis=1),
                            pltpu.roll(acc_hi, half, axis=1) / acc_hi)
            o_ref[...] = out.astype(BF16)

    @pl.when(step % 2 == 0)
    def _():
        run(set_a, set_b)

    @pl.when(step % 2 == 1)
    def _():
        run(set_b, set_a)


def _attention(proj, *, diff, batch, seq, row0, lam_init=0.0, lam=None, subln=None):
    tq = ATT_SCORE_BYTES // (4 * seq)
    nq = seq // tq
    q0 = row0 // tq
    s0 = row0 // seq
    n_units = DIFF_HEADS
    n_blocks = batch * n_units * nq
    kv_unit = (lambda h: h) if diff else (lambda h: h // 2)
    qc, kc, vc = ((COL_DQ, COL_DK, COL_DV) if diff else (COL_GQ, COL_GK, COL_GV))

    def block(g):
        return g // (n_units * nq), (g // nq) % n_units, g % nq

    def started(g):
        return block(jnp.minimum(g, n_blocks - 1))

    def finished(g):
        return block(jnp.maximum(g - 1, 0))

    def q_map(g):
        b, h, i = started(g)
        return q0 + b * nq + i, qc // LANES + h

    def k_map(g):
        b, h, _ = started(g)
        return s0 + b, kc // LANES + kv_unit(h)

    def v_map(g):
        b, h, _ = finished(g)
        return s0 + b, vc // LANES + kv_unit(h)

    def o_map(g):
        b, h, i = finished(g)
        return b * nq + i, h

    in_specs = [pl.BlockSpec((tq, LANES), q_map),
                pl.BlockSpec((seq, LANES), k_map),
                pl.BlockSpec((seq, LANES), v_map)]
    args = [proj, proj, proj]
    if diff:
        in_specs += [pl.BlockSpec((4, DIFF_HEAD_DIM), lambda g: (0, 0)),
                     pl.BlockSpec((1, LANES), lambda g: (0, 0))]
        args += [lam, subln]
    return pl.pallas_call(
        functools.partial(_attn_kernel, diff=diff, lam_init=lam_init, nq=nq),
        grid=(n_blocks + 1,),
        in_specs=in_specs,
        out_specs=pl.BlockSpec((tq, LANES), o_map),
        out_shape=jax.ShapeDtypeStruct((batch * seq, D_MODEL), BF16),
        scratch_shapes=(([pltpu.VMEM((tq, seq), F32)] * 2 + [pltpu.VMEM((tq, 1), F32)] * 2) * 2
                        + [pltpu.VMEM((seq, 2 * LANES if diff else LANES), BF16)]),
        compiler_params=_params(("arbitrary",)),
        name=("diff_attn" if diff else "gqa_attn") + f"_{seq}",
    )(*args)


def _merge_kernel(ga_ref, gb_ref, a_ref, b_ref, x_ref, mod_ref, w_ref, o_ref):
    merged = (ga_ref[...].astype(F32) * a_ref[...].astype(F32)
              + gb_ref[...].astype(F32) * b_ref[...].astype(F32))
    y = jnp.dot(merged.astype(BF16), w_ref[...], preferred_element_type=F32)
    o_ref[...] = x_ref[...] + mod_ref[0, 2:3, :] * y


def _merge(proj, a, b, x, mod, w_out):
    tm = 512
    row = lambda i: (i, 0)
    return pl.pallas_call(
        _merge_kernel,
        grid=(N_TOK // tm,),
        in_specs=[pl.BlockSpec((tm, D_MODEL), lambda i: (i, COL_GA // D_MODEL)),
                  pl.BlockSpec((tm, D_MODEL), lambda i: (i, COL_GB // D_MODEL)),
                  pl.BlockSpec((tm, D_MODEL), row),
                  pl.BlockSpec((tm, D_MODEL), row),
                  pl.BlockSpec((tm, D_MODEL), row),
                  pl.BlockSpec((1, N_MOD, D_MODEL), lambda i: (i * tm // MOD_CHUNK, 0, 0)),
                  pl.BlockSpec((D_MODEL, D_MODEL), lambda i: (0, 0))],
        out_specs=pl.BlockSpec((tm, D_MODEL), row),
        out_shape=jax.ShapeDtypeStruct((N_TOK, D_MODEL), F32),
        compiler_params=_params(("arbitrary",)),
        name="merge_out",
    )(proj, proj, a, b, x, mod, w_out)


def _ffn_kernel(x_ref, mod_ref, n2_ref, wg_ref, wu_ref, wd_ref, o_ref, h_scr):
    f = pl.program_id(1)

    @pl.when(f == 0)
    def _():
        h = _modulated_norm(x_ref[...], n2_ref[...], mod_ref[0, 3:4, :], mod_ref[0, 4:5, :])
        h_scr[...] = h.astype(BF16)
        o_ref[...] = jnp.zeros_like(o_ref)

    h = h_scr[...]
    g = jnp.dot(h, wg_ref[...], preferred_element_type=F32)
    u = jnp.dot(h, wu_ref[...], preferred_element_type=F32)
    act = (g * _sigmoid(g) * u).astype(BF16)
    o_ref[...] += jnp.dot(act, wd_ref[...], preferred_element_type=F32)

    @pl.when(f == pl.num_programs(1) - 1)
    def _():
        o_ref[...] = x_ref[...] + mod_ref[0, 5:6, :] * o_ref[...]


def _ffn(x, mod, norm2, wg, wu, wd):
    tm, tf = 1024, 256
    return pl.pallas_call(
        _ffn_kernel,
        grid=(N_TOK // tm, D_FF_DENSE // tf),
        in_specs=[pl.BlockSpec((tm, D_MODEL), lambda i, f: (i, 0)),
                  pl.BlockSpec((1, N_MOD, D_MODEL), lambda i, f: (i * tm // MOD_CHUNK, 0, 0)),
                  pl.BlockSpec((1, D_MODEL), lambda i, f: (0, 0)),
                  pl.BlockSpec((D_MODEL, tf), lambda i, f: (0, f)),
                  pl.BlockSpec((D_MODEL, tf), lambda i, f: (0, f)),
                  pl.BlockSpec((tf, D_MODEL), lambda i, f: (f, 0))],
        out_specs=pl.BlockSpec((tm, D_MODEL), lambda i, f: (i, 0)),
        out_shape=jax.ShapeDtypeStruct((N_TOK, D_MODEL), F32),
        scratch_shapes=[pltpu.VMEM((tm, D_MODEL), BF16)],
        compiler_params=_params(("arbitrary", "arbitrary")),
        name="ffn_dense",
    )(x, mod, norm2, wg, wu, wd)


MOE_TM = 1024


def _router_kernel(x_ref, mod_ref, n2_ref, rw_ref, ht_ref, gate_ref, rank_ref, rankt_ref, cnt_ref):
    h = _modulated_norm(x_ref[...], n2_ref[...], mod_ref[0, 3:4, :], mod_ref[0, 4:5, :])
    ht_ref[...] = h.T.astype(BF16)
    logits = jnp.dot(h, rw_ref[...], preferred_element_type=F32, precision=lax.Precision.HIGHEST)
    tm = h.shape[0]
    lane = lax.broadcasted_iota(jnp.int32, logits.shape, 1).astype(F32)
    neg = jnp.float32(-jnp.inf)
    lg = jnp.where(lane < N_EXPERTS, logits, neg)
    m1 = jnp.max(lg, axis=-1, keepdims=True)
    i1 = jnp.min(jnp.where(lg == m1, lane, float(LANES)), axis=-1, keepdims=True)
    lg2 = jnp.where(lane == i1, neg, lg)
    m2 = jnp.max(lg2, axis=-1, keepdims=True)
    i2 = jnp.min(jnp.where(lg2 == m2, lane, float(LANES)), axis=-1, keepdims=True)
    e2 = jnp.exp(m2 - m1)
    den = 1.0 + e2
    gate_ref[...] = jnp.where(lane == i1, 1.0 / den, 0.0) + jnp.where(lane == i2, e2 / den, 0.0)
    sel = (lane == i1) | (lane == i2)
    sel_f = jnp.where(sel, 1.0, 0.0)
    before = (lax.broadcasted_iota(jnp.int32, (tm, tm), 1)
              < lax.broadcasted_iota(jnp.int32, (tm, tm), 0))
    slot = jnp.dot(jnp.where(before, 1.0, 0.0).astype(BF16), sel_f.astype(BF16), preferred_element_type=F32)
    rank = jnp.where(sel, slot, -1.0)
    rank_ref[...] = rank
    rankt_ref[...] = rank.T[:SUBLANES, :]
    cnt = jnp.sum(sel_f, axis=0, keepdims=True)
    cnt_ref[...] = jnp.broadcast_to(cnt, (SUBLANES, LANES)).astype(jnp.int32)


def _router(x, mod, norm2, rw, *, tm=MOE_TM, mod_chunk=MOD_CHUNK):
    n, d = x.shape
    nt = n // tm
    return pl.pallas_call(
        _router_kernel,
        grid=(nt,),
        in_specs=[pl.BlockSpec((tm, d), lambda i: (i, 0)),
                  pl.BlockSpec((1, N_MOD, d), lambda i: (i * tm // mod_chunk, 0, 0)),
                  pl.BlockSpec((1, d), lambda i: (0, 0)),
                  pl.BlockSpec((d, LANES), lambda i: (0, 0))],
        out_specs=[pl.BlockSpec((d, tm), lambda i: (0, i)),
                   pl.BlockSpec((tm, LANES), lambda i: (i, 0)),
                   pl.BlockSpec((tm, LANES), lambda i: (i, 0)),
                   pl.BlockSpec((SUBLANES, tm), lambda i: (i, 0)),
                   pl.BlockSpec((SUBLANES, LANES), lambda i: (i, 0))],
        out_shape=[jax.ShapeDtypeStruct((d, n), BF16),
                   jax.ShapeDtypeStruct((n, LANES), F32),
                   jax.ShapeDtypeStruct((n, LANES), F32),
                   jax.ShapeDtypeStruct((nt * SUBLANES, tm), F32),
                   jax.ShapeDtypeStruct((nt * SUBLANES, LANES), jnp.int32)],
        compiler_params=_params(("arbitrary",)),
        name="moe_router",
    )(x, mod, norm2, rw)


def _moe_kernel(cnt_ref, ht_ref, rankt_ref, rank_ref, gate_ref, x_ref, mod_ref, wg_ref, wu_ref, wd_ref,
                o_ref, xt_scr, yt_scr):
    i, e, f = pl.program_id(0), pl.program_id(1), pl.program_id(2)
    last_f = pl.num_programs(2) - 1
    tm = o_ref.shape[0]
    n_blk = (cnt_ref[i * N_EXPERTS + e] + MXU_COLS - 1) // MXU_COLS

    @pl.when((e == 0) & (f == 0))
    def _():
        o_ref[...] = jnp.zeros_like(o_ref)

    @pl.when(f == 0)
    def _():
        slot = rankt_ref[pl.ds(e, 1), :].astype(jnp.int32)
        sub = lax.broadcasted_iota(jnp.int32, (MXU_COLS, tm), 0)

        def gather(b, carry):
            pick = jnp.where(sub + b * MXU_COLS == slot, 1.0, 0.0).astype(BF16)
            xt_scr[b] = lax.dot_general(ht_ref[...], pick, NT_DIMS, preferred_element_type=F32).astype(BF16)
            yt_scr[b] = jnp.zeros(yt_scr.shape[1:], F32)
            return carry

        lax.fori_loop(0, n_blk, gather, 0)

    def expert(b, carry):
        xb = xt_scr[b]
        g = jnp.dot(wg_ref[0], xb, preferred_element_type=F32)
        u = jnp.dot(wu_ref[0], xb, preferred_element_type=F32)
        act = (g * _sigmoid(g) * u).astype(BF16)
        yt_scr[b] += jnp.dot(wd_ref[0], act, preferred_element_type=F32)
        return carry

    lax.fori_loop(0, n_blk, expert, 0)

    @pl.when(f == last_f)
    def _():
        on_lane = lax.broadcasted_iota(jnp.int32, (tm, LANES), 1) == e
        slot = jnp.sum(jnp.where(on_lane, rank_ref[...], 0.0), axis=-1, keepdims=True).astype(jnp.int32)
        gate = jnp.sum(jnp.where(on_lane, gate_ref[...], 0.0), axis=-1, keepdims=True)
        col = lax.broadcasted_iota(jnp.int32, (tm, MXU_COLS), 1)

        def scatter(b, carry):
            pick = jnp.where(col + b * MXU_COLS == slot, 1.0, 0.0).astype(BF16)
            y = lax.dot_general(pick, yt_scr[b].astype(BF16), NT_DIMS, preferred_element_type=F32)
            o_ref[...] += gate * y
            return carry

        lax.fori_loop(0, n_blk, scatter, 0)

    @pl.when((e == pl.num_programs(1) - 1) & (f == last_f))
    def _():
        o_ref[...] = x_ref[...] + mod_ref[0, 5:6, :] * o_ref[...]


def _moe(counts, ht, rankt, rank, gate, x, mod, wgt, wut, wdt, *, tm=MOE_TM, tf=896, mod_chunk=MOD_CHUNK):
    n, d = x.shape
    n_exp, d_ff, _ = wgt.shape
    grid_spec = pltpu.PrefetchScalarGridSpec(
        num_scalar_prefetch=1,
        grid=(n // tm, n_exp, d_ff // tf),
        in_specs=[pl.BlockSpec((d, tm), lambda i, e, f, c: (0, i)),
                  pl.BlockSpec((SUBLANES, tm), lambda i, e, f, c: (i, 0)),
                  pl.BlockSpec((tm, LANES), lambda i, e, f, c: (i, 0)),
                  pl.BlockSpec((tm, LANES), lambda i, e, f, c: (i, 0)),
                  pl.BlockSpec((tm, d), lambda i, e, f, c: (i, 0)),
                  pl.BlockSpec((1, N_MOD, d), lambda i, e, f, c: (i * tm // mod_chunk, 0, 0)),
                  pl.BlockSpec((1, tf, d), lambda i, e, f, c: (e, f, 0)),
                  pl.BlockSpec((1, tf, d), lambda i, e, f, c: (e, f, 0)),
                  pl.BlockSpec((1, d, tf), lambda i, e, f, c: (e, 0, f))],
        out_specs=pl.BlockSpec((tm, d), lambda i, e, f, c: (i, 0)),
        scratch_shapes=[pltpu.VMEM((tm // MXU_COLS, d, MXU_COLS), BF16),
                        pltpu.VMEM((tm // MXU_COLS, d, MXU_COLS), F32)])
    return pl.pallas_call(
        _moe_kernel,
        grid_spec=grid_spec,
        out_shape=jax.ShapeDtypeStruct((n, d), F32),
        compiler_params=_params(("arbitrary", "arbitrary", "arbitrary")),
        name="moe_experts",
    )(counts, ht, rankt, rank, gate, x, mod, wgt, wut, wdt)


def _moe_layer(x, mod, norm2, router_w, w_gate, w_up, w_down):
    rw = jnp.pad(router_w, ((0, 0), (0, LANES - N_EXPERTS)))
    ht, gate, rank, rankt, cnt = _router(x, mod, norm2, rw)
    counts = cnt.reshape(-1, SUBLANES, LANES)[:, 0, :N_EXPERTS].reshape(-1)
    wgt = jnp.swapaxes(w_gate, 1, 2).astype(BF16)
    wut = jnp.swapaxes(w_up, 1, 2).astype(BF16)
    wdt = jnp.swapaxes(w_down, 1, 2).astype(BF16)
    return _moe(counts, ht, rankt, rank, gate, x, mod, wgt, wut, wdt)


def _final_kernel(x_ref, g_ref, o_ref):
    x = x_ref[...]
    ms = jnp.mean(x * x, axis=-1, keepdims=True)
    o_ref[...] = x * lax.rsqrt(ms + NORM_EPS) * g_ref[...]


def _final_norm(x, gain):
    tm = 1024
    return pl.pallas_call(
        _final_kernel,
        grid=(N_TOK // tm,),
        in_specs=[pl.BlockSpec((tm, D_MODEL), lambda i: (i, 0)),
                  pl.BlockSpec((1, D_MODEL), lambda i: (0, 0))],
        out_specs=pl.BlockSpec((tm, D_MODEL), lambda i: (i, 0)),
        out_shape=jax.ShapeDtypeStruct((N_TOK, D_MODEL), F32),
        compiler_params=_params(("arbitrary",)),
        name="final_norm",
    )(x, gain)


def _rope_tables():
    pos = jnp.arange(DEC_SEQ, dtype=jnp.int32)
    d = jnp.arange(LANES, dtype=jnp.int32) % DIFF_HEAD_DIM

    def angles(p, theta, idx, half):
        inv_freq = jnp.exp(-math.log(theta) * jnp.arange(half, dtype=F32) / half)
        return p.astype(F32)[:, None] * inv_freq[idx][None, :]

    half_d = DIFF_ROT_DIM // 2
    ang = angles(pos, ROPE_THETA, d % half_d, half_d)
    active = (d < DIFF_ROT_DIM)[None, :]
    first = (d < half_d)[None, :]
    tab_d = jnp.stack([jnp.where(active, jnp.cos(ang), 1.0),
                       jnp.where(first, -jnp.sin(ang), 0.0),
                       jnp.where(active & ~first, jnp.sin(ang), 0.0)])

    half_g = GQA_HEAD_DIM // 4
    use_row = (d < GQA_HEAD_DIM // 2)[None, :]
    ang_r = angles(pos // GRID_W, AXIAL_THETA, d % half_g, half_g)
    ang_c = angles(pos % GRID_W, AXIAL_THETA, d % half_g, half_g)
    ang = jnp.where(use_row, ang_r, ang_c)
    first = ((d % (2 * half_g)) < half_g)[None, :]
    tab_g = jnp.stack([jnp.cos(ang),
                       jnp.where(first, -jnp.sin(ang), 0.0),
                       jnp.where(first, 0.0, jnp.sin(ang))])
    return tab_d.astype(F32), tab_g.astype(F32)


def _extend_w_in(w):
    def dup(cols):
        c = cols.reshape(D_MODEL, GQA_KV_HEADS, 1, GQA_HEAD_DIM)
        return jnp.broadcast_to(c, (D_MODEL, GQA_KV_HEADS, 2, GQA_HEAD_DIM)).reshape(D_MODEL, -1)
    kv = GQA_KV_HEADS * GQA_HEAD_DIM
    gk = w[:, COL_GK:COL_GK + kv]
    gv = w[:, COL_GK + kv:COL_GK + 2 * kv]
    return jnp.concatenate([w[:, :COL_GK], dup(gk), dup(gv), w[:, COL_GK + 2 * kv:]], axis=1)


def kernel(x_prompt, x_sample, c_prompt, c_sample, w_ada, b_ada, norm1, w_in, diff_lambda, diff_subln,
           gqa_q_norm, gqa_k_norm, w_out, norm2, ffn_w_gate, ffn_w_up, ffn_w_down, router_w,
           moe_w_gate, moe_w_up, moe_w_down, final_norm):
    x = jnp.concatenate([x_prompt.reshape(N_PROMPT, D_MODEL), x_sample.reshape(N_SAMPLE, D_MODEL)], axis=0)
    n_c = BATCH + DEC_BATCH
    c_all = jnp.concatenate([c_prompt, c_sample, jnp.zeros((16 - n_c, D_MODEL), F32)], axis=0)
    mod_all = _ada(c_all, w_ada, b_ada)
    chunk_src = jnp.asarray(list(range(BATCH)) + [BATCH + i // 2 for i in range(2 * DEC_BATCH)], jnp.int32)
    tab_d, tab_g = _rope_tables()
    grp = jnp.kron(jnp.eye(MXU_COLS // GQA_HEAD_DIM, dtype=F32),
                   jnp.full((GQA_HEAD_DIM, GQA_HEAD_DIM), 1.0 / GQA_HEAD_DIM, F32)).astype(BF16)

    for l in range(DEPTH):
        mod = mod_all[l][chunk_src].reshape(N_CHUNKS, N_MOD, D_MODEL)
        lam_init = 0.8 - 0.6 * math.exp(-0.3 * l)
        w_ext = _extend_w_in(w_in[l]).astype(BF16)
        gains = jnp.stack([jnp.tile(gqa_q_norm[l], PROJ_TN // GQA_HEAD_DIM),
                           jnp.tile(gqa_k_norm[l], PROJ_TN // GQA_HEAD_DIM)])
        proj = _proj(x, mod, norm1[l][None, :], w_ext, tab_d, tab_g, gains, grp)
        subln = diff_subln[l][None, :]
        parts_a, parts_b = [], []
        for batch, seq, row0 in ((BATCH, SEQ, 0), (DEC_BATCH, DEC_SEQ, N_PROMPT)):
            parts_a.append(_attention(proj, diff=True, batch=batch, seq=seq, row0=row0, lam_init=lam_init,
                                      lam=diff_lambda[l], subln=subln))
            parts_b.append(_attention(proj, diff=False, batch=batch, seq=seq, row0=row0))
        a = jnp.concatenate(parts_a, axis=0)
        b = jnp.concatenate(parts_b, axis=0)
        x = _merge(proj, a, b, x, mod, w_out[l].astype(BF16))
        if l % 2 == 0:
            x = _ffn(x, mod, norm2[l][None, :], ffn_w_gate[l // 2].astype(BF16),
                     ffn_w_up[l // 2].astype(BF16), ffn_w_down[l // 2].astype(BF16))
        else:
            x = _moe_layer(x, mod, norm2[l][None, :], router_w[l // 2], moe_w_gate[l // 2],
                           moe_w_up[l // 2], moe_w_down[l // 2])
    y = _final_norm(x, final_norm[None, :])
    return (y[:N_PROMPT].reshape(BATCH, SEQ, D_MODEL), y[N_PROMPT:].reshape(DEC_BATCH, DEC_SEQ, D_MODEL))
```

```python
import functools
import math

import jax
import jax.numpy as jnp
from jax import lax
from jax.experimental import pallas as pl
from jax.experimental.pallas import tpu as pltpu

F32 = jnp.float32
BF16 = jnp.bfloat16

D_MODEL = 1024
BATCH = 8
SEQ = 4096
DEPTH = 4
DEC_BATCH = 2
DEC_SEQ = 8192
GRID_W = 64
DIFF_HEADS = 8
DIFF_HEAD_DIM = 64
DIFF_ROT_DIM = DIFF_HEAD_DIM // 4
ROPE_THETA = 500000.0
GQA_Q_HEADS = 16
GQA_KV_HEADS = 4
GQA_HEAD_DIM = 64
AXIAL_THETA = 10000.0
D_FF_DENSE = 2816
N_EXPERTS = 8
D_FF_EXPERT = 3584
NORM_EPS = 1e-6
N_MOD = 6

N_PROMPT = BATCH * SEQ
N_SAMPLE = DEC_BATCH * DEC_SEQ
N_TOK = N_PROMPT + N_SAMPLE
MOD_CHUNK = SEQ
N_CHUNKS = N_TOK // MOD_CHUNK

LANES = 128
SUBLANES = 8
MXU_COLS = 256
VMEM_LIMIT = 56 * 1024 * 1024

Q_SCALE = DIFF_HEAD_DIM ** -0.5 * math.log2(math.e)

PROJ_TN = 512
COL_DQ, COL_DK, COL_DV, COL_GQ = 0, 1024, 2048, 3072
COL_GK, COL_GV, COL_GA, COL_GB = 4096, 4608, 5120, 6144
PROJ_W = 7168
N_PROJ_TILES = PROJ_W // PROJ_TN

NT_DIMS = (((1,), (1,)), ((), ()))


def _params(sem, vmem=VMEM_LIMIT):
    return pltpu.CompilerParams(dimension_semantics=sem, vmem_limit_bytes=vmem)


def _sigmoid(x):
    return 0.5 * jnp.tanh(0.5 * x) + 0.5


def _ada_kernel(c_ref, w_ref, b_ref, o_ref):
    c = c_ref[...]
    ca = c * _sigmoid(c)
    o_ref[0] = jnp.dot(ca.astype(BF16), w_ref[0].astype(BF16), preferred_element_type=F32) + b_ref[0]


def _ada(c_all, w_ada, b_ada):
    rows = c_all.shape[0]
    tn = 1536
    width = N_MOD * D_MODEL
    return pl.pallas_call(
        _ada_kernel,
        grid=(DEPTH, width // tn),
        in_specs=[pl.BlockSpec((rows, D_MODEL), lambda l, j: (0, 0)),
                  pl.BlockSpec((1, D_MODEL, tn), lambda l, j: (l, 0, j)),
                  pl.BlockSpec((1, 1, tn), lambda l, j: (l, 0, j))],
        out_specs=pl.BlockSpec((1, rows, tn), lambda l, j: (l, 0, j)),
        out_shape=jax.ShapeDtypeStruct((DEPTH, rows, width), F32),
        compiler_params=_params(("arbitrary", "arbitrary")),
        name="ada_mod",
    )(c_all, w_ada, b_ada.reshape(DEPTH, 1, width))


def _modulated_norm(x, gain, shift, scale):
    ms = jnp.mean(x * x, axis=-1, keepdims=True)
    y = x * lax.rsqrt(ms + NORM_EPS) * gain
    return y * (1.0 + scale) + shift


def _rotate(x, c, sm, sp, shift):
    return x * c + pltpu.roll(x, LANES - shift, axis=1) * sm + pltpu.roll(x, shift, axis=1) * sp


def _pos_block(i, tm):
    return jnp.where(i < N_PROMPT // tm, i % (SEQ // tm), i % (DEC_SEQ // tm))


def _proj_kernel(x_ref, mod_ref, n1_ref, w_ref, tabd_ref, tabg_ref, gain_ref, grp_ref, o_ref, h_scr, acc_a, acc_b):
    j = pl.program_id(1)
    tile = j - 1
    n_slab = PROJ_TN // LANES
    q_tiles = DIFF_HEADS * 2 * DIFF_HEAD_DIM // PROJ_TN

    def matmul(dst):
        dst[...] = jnp.dot(h_scr[...], w_ref[...], preferred_element_type=F32)

    def rope_epilogue(src):
        c, sm, sp = tabd_ref[0], tabd_ref[1], tabd_ref[2]
        scale = jnp.where(tile < q_tiles, Q_SCALE, 1.0)
        for s in range(n_slab):
            xs = src[:, s * LANES:(s + 1) * LANES] * scale
            o_ref[:, s * LANES:(s + 1) * LANES] = _rotate(xs, c, sm, sp, DIFF_ROT_DIM // 2).astype(BF16)

    def plain_epilogue(src):
        o_ref[...] = src[...].astype(BF16)

    def norm_rope_epilogue(src):
        is_q = tile < COL_GK // PROJ_TN
        gain = gain_ref[pl.ds(jnp.where(is_q, 0, 1), 1), :] * jnp.where(is_q, Q_SCALE, 1.0)
        c, sm, sp = tabg_ref[0], tabg_ref[1], tabg_ref[2]
        for t in range(PROJ_TN // MXU_COLS):
            a = src[:, t * MXU_COLS:(t + 1) * MXU_COLS]
            sq = a * a
            hi = sq.astype(BF16)
            lo = (sq - hi.astype(F32)).astype(BF16)
            ms = (jnp.dot(hi, grp_ref[...], preferred_element_type=F32)
                  + jnp.dot(lo, grp_ref[...], preferred_element_type=F32))
            y = a * lax.rsqrt(ms + NORM_EPS) * gain[:, t * MXU_COLS:(t + 1) * MXU_COLS]
            for s in range(MXU_COLS // LANES):
                col = t * MXU_COLS + s * LANES
                ys = y[:, s * LANES:(s + 1) * LANES]
                o_ref[:, col:col + LANES] = _rotate(ys, c, sm, sp, GQA_HEAD_DIM // 4).astype(BF16)

    def gate_epilogue(src):
        for s in range(n_slab):
            o_ref[:, s * LANES:(s + 1) * LANES] = _sigmoid(src[:, s * LANES:(s + 1) * LANES]).astype(BF16)

    is_plain = ((tile >= COL_DV // PROJ_TN) & (tile < COL_GQ // PROJ_TN)) | (tile == COL_GV // PROJ_TN)
    kinds = ((tile < COL_DV // PROJ_TN, rope_epilogue),
             (is_plain, plain_epilogue),
             ((tile >= COL_GQ // PROJ_TN) & (tile < COL_GV // PROJ_TN), norm_rope_epilogue),
             (tile >= COL_GA // PROJ_TN, gate_epilogue))

    @pl.when(j == 0)
    def _():
        h = _modulated_norm(x_ref[...], n1_ref[...], mod_ref[0, 0:1, :], mod_ref[0, 1:2, :])
        h_scr[...] = h.astype(BF16)
        matmul(acc_a)

    for parity, cur, prev in ((0, acc_a, acc_b), (1, acc_b, acc_a)):
        for cond, epilogue in kinds:
            @pl.when((j > 0) & (j < N_PROJ_TILES) & (j % 2 == parity) & cond)
            def _(cur=cur, prev=prev, epilogue=epilogue):
                matmul(cur)
                epilogue(prev)

    @pl.when(j == N_PROJ_TILES)
    def _():
        gate_epilogue(acc_a if N_PROJ_TILES % 2 else acc_b)


def _proj(x, mod, norm1, w_ext, tab_d, tab_g, gains, grp):
    tm = 1024
    last = N_PROJ_TILES - 1
    return pl.pallas_call(
        _proj_kernel,
        grid=(N_TOK // tm, N_PROJ_TILES + 1),
        in_specs=[pl.BlockSpec((tm, D_MODEL), lambda i, j: (i, 0)),
                  pl.BlockSpec((1, N_MOD, D_MODEL), lambda i, j: (i * tm // MOD_CHUNK, 0, 0)),
                  pl.BlockSpec((1, D_MODEL), lambda i, j: (0, 0)),
                  pl.BlockSpec((D_MODEL, PROJ_TN), lambda i, j: (0, jnp.minimum(j, last))),
                  pl.BlockSpec((3, tm, LANES), lambda i, j: (0, _pos_block(i, tm), 0)),
                  pl.BlockSpec((3, tm, LANES), lambda i, j: (0, _pos_block(i, tm), 0)),
                  pl.BlockSpec((2, PROJ_TN), lambda i, j: (0, 0)),
                  pl.BlockSpec((MXU_COLS, MXU_COLS), lambda i, j: (0, 0))],
        out_specs=pl.BlockSpec((tm, PROJ_TN), lambda i, j: (i, jnp.maximum(j - 1, 0))),
        out_shape=jax.ShapeDtypeStruct((N_TOK, PROJ_W), BF16),
        scratch_shapes=[pltpu.VMEM((tm, D_MODEL), BF16), pltpu.VMEM((tm, PROJ_TN), F32),
                        pltpu.VMEM((tm, PROJ_TN), F32)],
        compiler_params=_params(("arbitrary", "arbitrary")),
        name="proj_in",
    )(x, mod, norm1, w_ext, tab_d, tab_g, gains, grp)


ATT_SCORE_BYTES = 8 * 1024 * 1024


def _attn_kernel(*refs, diff, lam_init, nq, aliased):
    q_ref, k_ref, v_ref = refs[:3]
    if diff:
        lam_ref, subln_ref = refs[3:5]
    n_in = 3 + (2 if diff else 0) + (1 if aliased else 0)
    o_ref, bufs, vx_ref = refs[n_in], refs[n_in + 1:-1], refs[-1]
    set_a, set_b = bufs[:4], bufs[4:]
    half = LANES // 2
    step = pl.program_id(0)

    @pl.when(step == 0)
    def _():
        for ref in set_b:
            ref[...] = jnp.zeros_like(ref)

    @pl.when(jnp.maximum(step - 1, 0) % nq == 0)
    def _():
        v = v_ref[...]
        if diff:
            vx_ref[:, :LANES] = v
            vx_ref[:, LANES:] = jnp.ones_like(v)
        else:
            lane = lax.broadcasted_iota(jnp.int32, v.shape, 1)
            vx_ref[...] = jnp.where(lane < half, v, jnp.ones_like(v))

    def finish(s_ref, m_ref):
        p = jnp.exp2((s_ref[...] - m_ref[...]).astype(BF16))
        acc = jnp.dot(p, vx_ref[...], preferred_element_type=F32)
        return acc[:, :LANES] / acc[:, LANES:] if diff else acc

    def start(qh, s_ref, m_ref):
        s = lax.dot_general(qh, k_ref[...], NT_DIMS, preferred_element_type=F32)
        s_ref[...] = s
        m_ref[...] = jnp.max(s, axis=-1, keepdims=True)

    def run(cur, prev):
        q = q_ref[...]
        low = lax.broadcasted_iota(jnp.int32, q.shape, 1) < half
        zero = jnp.zeros_like(q)
        acc_lo = finish(prev[0], prev[2])
        start(jnp.where(low, q, zero), cur[0], cur[2])
        acc_hi = finish(prev[1], prev[3])
        start(jnp.where(low, zero, q), cur[1], cur[3])
        if diff:
            lp = lam_ref[...]
            t1 = jnp.sum(lp[0:1] * lp[1:2], axis=-1, keepdims=True)
            t2 = jnp.sum(lp[2:3] * lp[3:4], axis=-1, keepdims=True)
            lam = jnp.exp(t1) - jnp.exp(t2) + lam_init
            a = acc_lo - lam * acc_hi
            ms = jnp.mean(a * a, axis=-1, keepdims=True)
            a = a * lax.rsqrt(ms + NORM_EPS) * subln_ref[...] * (1.0 - lam_init)
            o_ref[...] = a.astype(BF16)
        else:
            out = jnp.where(low, acc_lo / pltpu.roll(acc_lo, half, axis=1),
                            pltpu.roll(acc_hi, half, axis=1) / acc_hi)
            o_ref[...] = out.astype(BF16)

    @pl.when(step % 2 == 0)
    def _():
        run(set_a, set_b)

    @pl.when(step % 2 == 1)
    def _():
        run(set_b, set_a)


def _attention(proj, *, diff, batch, seq, row0, lam_init=0.0, lam=None, subln=None, out=None):
    tq = ATT_SCORE_BYTES // (4 * seq)
    nq = seq // tq
    q0 = row0 // tq
    s0 = row0 // seq
    n_units = DIFF_HEADS
    n_blocks = batch * n_units * nq
    kv_unit = (lambda h: h) if diff else (lambda h: h // 2)
    qc, kc, vc = ((COL_DQ, COL_DK, COL_DV) if diff else (COL_GQ, COL_GK, COL_GV))

    def block(g):
        return g // (n_units * nq), (g // nq) % n_units, g % nq

    def started(g):
        return block(jnp.minimum(g, n_blocks - 1))

    def finished(g):
        return block(jnp.maximum(g - 1, 0))

    def q_map(g):
        b, h, i = started(g)
        return q0 + b * nq + i, qc // LANES + h

    def k_map(g):
        b, h, _ = started(g)
        return s0 + b, kc // LANES + kv_unit(h)

    def v_map(g):
        b, h, _ = finished(g)
        return s0 + b, vc // LANES + kv_unit(h)

    def o_map(g):
        b, h, i = finished(g)
        return q0 + b * nq + i, h

    in_specs = [pl.BlockSpec((tq, LANES), q_map),
                pl.BlockSpec((seq, LANES), k_map),
                pl.BlockSpec((seq, LANES), v_map)]
    args = [proj, proj, proj]
    if diff:
        in_specs += [pl.BlockSpec((4, DIFF_HEAD_DIM), lambda g: (0, 0)),
                     pl.BlockSpec((1, LANES), lambda g: (0, 0))]
        args += [lam, subln]
    aliases = {}
    if out is not None:
        aliases = {len(args): 0}
        in_specs.append(pl.BlockSpec(memory_space=pl.ANY))
        args.append(out)
    return pl.pallas_call(
        functools.partial(_attn_kernel, diff=diff, lam_init=lam_init, nq=nq, aliased=out is not None),
        grid=(n_blocks + 1,),
        in_specs=in_specs,
        out_specs=pl.BlockSpec((tq, LANES), o_map),
        out_shape=jax.ShapeDtypeStruct((N_TOK, D_MODEL), BF16),
        input_output_aliases=aliases,
        scratch_shapes=(([pltpu.VMEM((tq, seq), F32)] * 2 + [pltpu.VMEM((tq, 1), F32)] * 2) * 2
                        + [pltpu.VMEM((seq, 2 * LANES if diff else LANES), BF16)]),
        compiler_params=_params(("arbitrary",)),
        name=("diff_attn" if diff else "gqa_attn") + f"_{seq}",
    )(*args)


def _merge_kernel(ga_ref, gb_ref, a_ref, b_ref, x_ref, mod_ref, w_ref, o_ref):
    merged = (ga_ref[...].astype(F32) * a_ref[...].astype(F32)
              + gb_ref[...].astype(F32) * b_ref[...].astype(F32))
    y = jnp.dot(merged.astype(BF16), w_ref[...], preferred_element_type=F32)
    o_ref[...] = x_ref[...] + mod_ref[0, 2:3, :] * y


def _merge(proj, a, b, x, mod, w_out):
    tm = 512
    row = lambda i: (i, 0)
    return pl.pallas_call(
        _merge_kernel,
        grid=(N_TOK // tm,),
        in_specs=[pl.BlockSpec((tm, D_MODEL), lambda i: (i, COL_GA // D_MODEL)),
                  pl.BlockSpec((tm, D_MODEL), lambda i: (i, COL_GB // D_MODEL)),
                  pl.BlockSpec((tm, D_MODEL), row),
                  pl.BlockSpec((tm, D_MODEL), row),
                  pl.BlockSpec((tm, D_MODEL), row),
                  pl.BlockSpec((1, N_MOD, D_MODEL), lambda i: (i * tm // MOD_CHUNK, 0, 0)),
                  pl.BlockSpec((D_MODEL, D_MODEL), lambda i: (0, 0))],
        out_specs=pl.BlockSpec((tm, D_MODEL), row),
        out_shape=jax.ShapeDtypeStruct((N_TOK, D_MODEL), F32),
        compiler_params=_params(("arbitrary",)),
        name="merge_out",
    )(proj, proj, a, b, x, mod, w_out)


def _ffn_kernel(x_ref, mod_ref, n2_ref, wg_ref, wu_ref, wd_ref, o_ref, h_scr):
    f = pl.program_id(1)

    @pl.when(f == 0)
    def _():
        h = _modulated_norm(x_ref[...], n2_ref[...], mod_ref[0, 3:4, :], mod_ref[0, 4:5, :])
        h_scr[...] = h.astype(BF16)
        o_ref[...] = jnp.zeros_like(o_ref)

    h = h_scr[...]
    g = jnp.dot(h, wg_ref[...], preferred_element_type=F32)
    u = jnp.dot(h, wu_ref[...], preferred_element_type=F32)
    act = (g * _sigmoid(g) * u).astype(BF16)
    o_ref[...] += jnp.dot(act, wd_ref[...], preferred_element_type=F32)

    @pl.when(f == pl.num_programs(1) - 1)
    def _():
        o_ref[...] = x_ref[...] + mod_ref[0, 5:6, :] * o_ref[...]


def _ffn(x, mod, norm2, wg, wu, wd):
    tm, tf = 1024, 256
    return pl.pallas_call(
        _ffn_kernel,
        grid=(N_TOK // tm, D_FF_DENSE // tf),
        in_specs=[pl.BlockSpec((tm, D_MODEL), lambda i, f: (i, 0)),
                  pl.BlockSpec((1, N_MOD, D_MODEL), lambda i, f: (i * tm // MOD_CHUNK, 0, 0)),
                  pl.BlockSpec((1, D_MODEL), lambda i, f: (0, 0)),
                  pl.BlockSpec((D_MODEL, tf), lambda i, f: (0, f)),
                  pl.BlockSpec((D_MODEL, tf), lambda i, f: (0, f)),
                  pl.BlockSpec((tf, D_MODEL), lambda i, f: (f, 0))],
        out_specs=pl.BlockSpec((tm, D_MODEL), lambda i, f: (i, 0)),
        out_shape=jax.ShapeDtypeStruct((N_TOK, D_MODEL), F32),
        scratch_shapes=[pltpu.VMEM((tm, D_MODEL), BF16)],
        compiler_params=_params(("arbitrary", "arbitrary")),
        name="ffn_dense",
    )(x, mod, norm2, wg, wu, wd)


MOE_TM = 1024


def _router_kernel(x_ref, mod_ref, n2_ref, rw_ref, ht_ref, gate_ref, rank_ref, rankt_ref, cnt_ref):
    h = _modulated_norm(x_ref[...], n2_ref[...], mod_ref[0, 3:4, :], mod_ref[0, 4:5, :])
    ht_ref[...] = h.T.astype(BF16)
    logits = jnp.dot(h, rw_ref[...], preferred_element_type=F32, precision=lax.Precision.HIGHEST)
    tm = h.shape[0]
    lane = lax.broadcasted_iota(jnp.int32, logits.shape, 1).astype(F32)
    neg = jnp.float32(-jnp.inf)
    lg = jnp.where(lane < N_EXPERTS, logits, neg)
    m1 = jnp.max(lg, axis=-1, keepdims=True)
    i1 = jnp.min(jnp.where(lg == m1, lane, float(LANES)), axis=-1, keepdims=True)
    lg2 = jnp.where(lane == i1, neg, lg)
    m2 = jnp.max(lg2, axis=-1, keepdims=True)
    i2 = jnp.min(jnp.where(lg2 == m2, lane, float(LANES)), axis=-1, keepdims=True)
    e2 = jnp.exp(m2 - m1)
    den = 1.0 + e2
    gate_ref[...] = jnp.where(lane == i1, 1.0 / den, 0.0) + jnp.where(lane == i2, e2 / den, 0.0)
    sel = (lane == i1) | (lane == i2)
    sel_f = jnp.where(sel, 1.0, 0.0)
    before = (lax.broadcasted_iota(jnp.int32, (tm, tm), 1)
              < lax.broadcasted_iota(jnp.int32, (tm, tm), 0))
    slot = jnp.dot(jnp.where(before, 1.0, 0.0).astype(BF16), sel_f.astype(BF16), preferred_element_type=F32)
    rank = jnp.where(sel, slot, -1.0)
    rank_ref[...] = rank
    rankt_ref[...] = rank.T[:SUBLANES, :]
    cnt = jnp.sum(sel_f, axis=0, keepdims=True)
    cnt_ref[...] = jnp.broadcast_to(cnt, (SUBLANES, LANES)).astype(jnp.int32)


def _router(x, mod, norm2, rw, *, tm=MOE_TM, mod_chunk=MOD_CHUNK):
    n, d = x.shape
    nt = n // tm
    return pl.pallas_call(
        _router_kernel,
        grid=(nt,),
        in_specs=[pl.BlockSpec((tm, d), lambda i: (i, 0)),
                  pl.BlockSpec((1, N_MOD, d), lambda i: (i * tm // mod_chunk, 0, 0)),
                  pl.BlockSpec((1, d), lambda i: (0, 0)),
                  pl.BlockSpec((d, LANES), lambda i: (0, 0))],
        out_specs=[pl.BlockSpec((d, tm), lambda i: (0, i)),
                   pl.BlockSpec((tm, LANES), lambda i: (i, 0)),
                   pl.BlockSpec((tm, LANES), lambda i: (i, 0)),
                   pl.BlockSpec((SUBLANES, tm), lambda i: (i, 0)),
                   pl.BlockSpec((SUBLANES, LANES), lambda i: (i, 0))],
        out_shape=[jax.ShapeDtypeStruct((d, n), BF16),
                   jax.ShapeDtypeStruct((n, LANES), F32),
                   jax.ShapeDtypeStruct((n, LANES), F32),
                   jax.ShapeDtypeStruct((nt * SUBLANES, tm), F32),
                   jax.ShapeDtypeStruct((nt * SUBLANES, LANES), jnp.int32)],
        compiler_params=_params(("arbitrary",)),
        name="moe_router",
    )(x, mod, norm2, rw)


def _moe_kernel(cnt_ref, ht_ref, rankt_ref, rank_ref, gate_ref, x_ref, mod_ref, wg_ref, wu_ref, wd_ref,
                o_ref, xt_scr, yt_scr):
    i, e, f = pl.program_id(0), pl.program_id(1), pl.program_id(2)
    last_f = pl.num_programs(2) - 1
    tm = o_ref.shape[0]
    n_blk = (cnt_ref[i * N_EXPERTS + e] + MXU_COLS - 1) // MXU_COLS

    @pl.when((e == 0) & (f == 0))
    def _():
        o_ref[...] = jnp.zeros_like(o_ref)

    @pl.when(f == 0)
    def _():
        slot = rankt_ref[pl.ds(e, 1), :].astype(jnp.int32)
        sub = lax.broadcasted_iota(jnp.int32, (MXU_COLS, tm), 0)

        def gather(b, carry):
            pick = jnp.where(sub + b * MXU_COLS == slot, 1.0, 0.0).astype(BF16)
            xt_scr[b] = lax.dot_general(ht_ref[...], pick, NT_DIMS, preferred_element_type=F32).astype(BF16)
            yt_scr[b] = jnp.zeros(yt_scr.shape[1:], F32)
            return carry

        lax.fori_loop(0, n_blk, gather, 0)

    def expert(b, carry):
        xb = xt_scr[b]
        g = jnp.dot(wg_ref[0], xb, preferred_element_type=F32)
        u = jnp.dot(wu_ref[0], xb, preferred_element_type=F32)
        act = (g * _sigmoid(g) * u).astype(BF16)
        yt_scr[b] += jnp.dot(wd_ref[0], act, preferred_element_type=F32)
        return carry

    lax.fori_loop(0, n_blk, expert, 0)

    @pl.when(f == last_f)
    def _():
        on_lane = lax.broadcasted_iota(jnp.int32, (tm, LANES), 1) == e
        slot = jnp.sum(jnp.where(on_lane, rank_ref[...], 0.0), axis=-1, keepdims=True).astype(jnp.int32)
        gate = jnp.sum(jnp.where(on_lane, gate_ref[...], 0.0), axis=-1, keepdims=True)
        col = lax.broadcasted_iota(jnp.int32, (tm, MXU_COLS), 1)

        def scatter(b, carry):
            pick = jnp.where(col + b * MXU_COLS == slot, 1.0, 0.0).astype(BF16)
            y = lax.dot_general(pick, yt_scr[b].astype(BF16), NT_DIMS, preferred_element_type=F32)
            o_ref[...] += gate * y
            return carry

        lax.fori_loop(0, n_blk, scatter, 0)

    @pl.when((e == pl.num_programs(1) - 1) & (f == last_f))
    def _():
        o_ref[...] = x_ref[...] + mod_ref[0, 5:6, :] * o_ref[...]


def _moe(counts, ht, rankt, rank, gate, x, mod, wgt, wut, wdt, *, tm=MOE_TM, tf=896, mod_chunk=MOD_CHUNK):
    n, d = x.shape
    n_exp, d_ff, _ = wgt.shape
    grid_spec = pltpu.PrefetchScalarGridSpec(
        num_scalar_prefetch=1,
        grid=(n // tm, n_exp, d_ff // tf),
        in_specs=[pl.BlockSpec((d, tm), lambda i, e, f, c: (0, i)),
                  pl.BlockSpec((SUBLANES, tm), lambda i, e, f, c: (i, 0)),
                  pl.BlockSpec((tm, LANES), lambda i, e, f, c: (i, 0)),
                  pl.BlockSpec((tm, LANES), lambda i, e, f, c: (i, 0)),
                  pl.BlockSpec((tm, d), lambda i, e, f, c: (i, 0)),
                  pl.BlockSpec((1, N_MOD, d), lambda i, e, f, c: (i * tm // mod_chunk, 0, 0)),
                  pl.BlockSpec((1, tf, d), lambda i, e, f, c: (e, f, 0)),
                  pl.BlockSpec((1, tf, d), lambda i, e, f, c: (e, f, 0)),
                  pl.BlockSpec((1, d, tf), lambda i, e, f, c: (e, 0, f))],
        out_specs=pl.BlockSpec((tm, d), lambda i, e, f, c: (i, 0)),
        scratch_shapes=[pltpu.VMEM((tm // MXU_COLS, d, MXU_COLS), BF16),
                        pltpu.VMEM((tm // MXU_COLS, d, MXU_COLS), F32)])
    return pl.pallas_call(
        _moe_kernel,
        grid_spec=grid_spec,
        out_shape=jax.ShapeDtypeStruct((n, d), F32),
        compiler_params=_params(("arbitrary", "arbitrary", "arbitrary")),
        name="moe_experts",
    )(counts, ht, rankt, rank, gate, x, mod, wgt, wut, wdt)


def _moe_layer(x, mod, norm2, router_w, w_gate, w_up, w_down):
    rw = jnp.pad(router_w, ((0, 0), (0, LANES - N_EXPERTS)))
    ht, gate, rank, rankt, cnt = _router(x, mod, norm2, rw)
    counts = cnt.reshape(-1, SUBLANES, LANES)[:, 0, :N_EXPERTS].reshape(-1)
    wgt = jnp.swapaxes(w_gate, 1, 2).astype(BF16)
    wut = jnp.swapaxes(w_up, 1, 2).astype(BF16)
    wdt = jnp.swapaxes(w_down, 1, 2).astype(BF16)
    return _moe(counts, ht, rankt, rank, gate, x, mod, wgt, wut, wdt)


def _final_kernel(x_ref, g_ref, o_ref):
    x = x_ref[...]
    ms = jnp.mean(x * x, axis=-1, keepdims=True)
    o_ref[...] = x * lax.rsqrt(ms + NORM_EPS) * g_ref[...]


def _final_norm(x, gain):
    tm = 1024
    return pl.pallas_call(
        _final_kernel,
        grid=(N_TOK // tm,),
        in_specs=[pl.BlockSpec((tm, D_MODEL), lambda i: (i, 0)),
                  pl.BlockSpec((1, D_MODEL), lambda i: (0, 0))],
        out_specs=pl.BlockSpec((tm, D_MODEL), lambda i: (i, 0)),
        out_shape=jax.ShapeDtypeStruct((N_TOK, D_MODEL), F32),
        compiler_params=_params(("arbitrary",)),
        name="final_norm",
    )(x, gain)


def _rope_tables():
    pos = jnp.arange(DEC_SEQ, dtype=jnp.int32)
    d = jnp.arange(LANES, dtype=jnp.int32) % DIFF_HEAD_DIM

    def angles(p, theta, idx, half):
        inv_freq = jnp.exp(-math.log(theta) * jnp.arange(half, dtype=F32) / half)
        return p.astype(F32)[:, None] * inv_freq[idx][None, :]

    half_d = DIFF_ROT_DIM // 2
    ang = angles(pos, ROPE_THETA, d % half_d, half_d)
    active = (d < DIFF_ROT_DIM)[None, :]
    first = (d < half_d)[None, :]
    tab_d = jnp.stack([jnp.where(active, jnp.cos(ang), 1.0),
                       jnp.where(first, -jnp.sin(ang), 0.0),
                       jnp.where(active & ~first, jnp.sin(ang), 0.0)])

    half_g = GQA_HEAD_DIM // 4
    use_row = (d < GQA_HEAD_DIM // 2)[None, :]
    ang_r = angles(pos // GRID_W, AXIAL_THETA, d % half_g, half_g)
    ang_c = angles(pos % GRID_W, AXIAL_THETA, d % half_g, half_g)
    ang = jnp.where(use_row, ang_r, ang_c)
    first = ((d % (2 * half_g)) < half_g)[None, :]
    tab_g = jnp.stack([jnp.cos(ang),
                       jnp.where(first, -jnp.sin(ang), 0.0),
                       jnp.where(first, 0.0, jnp.sin(ang))])
    return tab_d.astype(F32), tab_g.astype(F32)


def _extend_w_in(w):
    def dup(cols):
        c = cols.reshape(D_MODEL, GQA_KV_HEADS, 1, GQA_HEAD_DIM)
        return jnp.broadcast_to(c, (D_MODEL, GQA_KV_HEADS, 2, GQA_HEAD_DIM)).reshape(D_MODEL, -1)
    kv = GQA_KV_HEADS * GQA_HEAD_DIM
    gk = w[:, COL_GK:COL_GK + kv]
    gv = w[:, COL_GK + kv:COL_GK + 2 * kv]
    return jnp.concatenate([w[:, :COL_GK], dup(gk), dup(gv), w[:, COL_GK + 2 * kv:]], axis=1)


def kernel(x_prompt, x_sample, c_prompt, c_sample, w_ada, b_ada, norm1, w_in, diff_lambda, diff_subln,
           gqa_q_norm, gqa_k_norm, w_out, norm2, ffn_w_gate, ffn_w_up, ffn_w_down, router_w,
           moe_w_gate, moe_w_up, moe_w_down, final_norm):
    x = jnp.concatenate([x_prompt.reshape(N_PROMPT, D_MODEL), x_sample.reshape(N_SAMPLE, D_MODEL)], axis=0)
    n_c = BATCH + DEC_BATCH
    c_all = jnp.concatenate([c_prompt, c_sample, jnp.zeros((16 - n_c, D_MODEL), F32)], axis=0)
    mod_all = _ada(c_all, w_ada, b_ada)
    chunk_src = jnp.asarray(list(range(BATCH)) + [BATCH + i // 2 for i in range(2 * DEC_BATCH)], jnp.int32)
    tab_d, tab_g = _rope_tables()
    grp = jnp.kron(jnp.eye(MXU_COLS // GQA_HEAD_DIM, dtype=F32),
                   jnp.full((GQA_HEAD_DIM, GQA_HEAD_DIM), 1.0 / GQA_HEAD_DIM, F32)).astype(BF16)

    for l in range(DEPTH):
        mod = mod_all[l][chunk_src].reshape(N_CHUNKS, N_MOD, D_MODEL)
        lam_init = 0.8 - 0.6 * math.exp(-0.3 * l)
        w_ext = _extend_w_in(w_in[l]).astype(BF16)
        gains = jnp.stack([jnp.tile(gqa_q_norm[l], PROJ_TN // GQA_HEAD_DIM),
                           jnp.tile(gqa_k_norm[l], PROJ_TN // GQA_HEAD_DIM)])
        proj = _proj(x, mod, norm1[l][None, :], w_ext, tab_d, tab_g, gains, grp)
        subln = diff_subln[l][None, :]
        a, b = None, None
        for batch, seq, row0 in ((BATCH, SEQ, 0), (DEC_BATCH, DEC_SEQ, N_PROMPT)):
            a = _attention(proj, diff=True, batch=batch, seq=seq, row0=row0, lam_init=lam_init,
                           lam=diff_lambda[l], subln=subln, out=a)
            b = _attention(proj, diff=False, batch=batch, seq=seq, row0=row0, out=b)
        x = _merge(proj, a, b, x, mod, w_out[l].astype(BF16))
        if l % 2 == 0:
            x = _ffn(x, mod, norm2[l][None, :], ffn_w_gate[l // 2].astype(BF16),
                     ffn_w_up[l // 2].astype(BF16), ffn_w_down[l // 2].astype(BF16))
        else:
            x = _moe_layer(x, mod, norm2[l][None, :], router_w[l // 2], moe_w_gate[l // 2],
                           moe_w_up[l // 2], moe_w_down[l // 2])
    y = _final_norm(x, final_norm[None, :])
    return (y[:N_PROMPT].reshape(BATCH, SEQ, D_MODEL), y[N_PROMPT:].reshape(DEC_BATCH, DEC_SEQ, D_MODEL))
```

```python
import functools
import math

import jax
import jax.numpy as jnp
from jax import lax
from jax.experimental import pallas as pl
from jax.experimental.pallas import tpu as pltpu

F32 = jnp.float32
BF16 = jnp.bfloat16

D_MODEL = 1024
BATCH = 8
SEQ = 4096
DEPTH = 4
DEC_BATCH = 2
DEC_SEQ = 8192
GRID_W = 64
DIFF_HEADS = 8
DIFF_HEAD_DIM = 64
DIFF_ROT_DIM = DIFF_HEAD_DIM // 4
ROPE_THETA = 500000.0
GQA_Q_HEADS = 16
GQA_KV_HEADS = 4
GQA_HEAD_DIM = 64
AXIAL_THETA = 10000.0
D_FF_DENSE = 2816
N_EXPERTS = 8
D_FF_EXPERT = 3584
NORM_EPS = 1e-6
N_MOD = 6

N_PROMPT = BATCH * SEQ
N_SAMPLE = DEC_BATCH * DEC_SEQ
N_TOK = N_PROMPT + N_SAMPLE
MOD_CHUNK = SEQ
N_CHUNKS = N_TOK // MOD_CHUNK

LANES = 128
SUBLANES = 8
MXU_COLS = 256
VMEM_LIMIT = 56 * 1024 * 1024

Q_SCALE = DIFF_HEAD_DIM ** -0.5 * math.log2(math.e)

PROJ_TN = 512
COL_DQ, COL_DK, COL_DV, COL_GQ = 0, 1024, 2048, 3072
COL_GK, COL_GV, COL_GA, COL_GB = 4096, 4608, 5120, 6144
PROJ_W = 7168
N_PROJ_TILES = PROJ_W // PROJ_TN

NT_DIMS = (((1,), (1,)), ((), ()))


def _params(sem, vmem=VMEM_LIMIT):
    return pltpu.CompilerParams(dimension_semantics=sem, vmem_limit_bytes=vmem)


def _sigmoid(x):
    return 0.5 * jnp.tanh(0.5 * x) + 0.5


def _ada_kernel(c_ref, w_ref, b_ref, o_ref):
    c = c_ref[...]
    ca = c * _sigmoid(c)
    o_ref[0] = jnp.dot(ca.astype(BF16), w_ref[0].astype(BF16), preferred_element_type=F32) + b_ref[0]


def _ada(c_all, w_ada, b_ada):
    rows = c_all.shape[0]
    tn = 1536
    width = N_MOD * D_MODEL
    return pl.pallas_call(
        _ada_kernel,
        grid=(DEPTH, width // tn),
        in_specs=[pl.BlockSpec((rows, D_MODEL), lambda l, j: (0, 0)),
                  pl.BlockSpec((1, D_MODEL, tn), lambda l, j: (l, 0, j)),
                  pl.BlockSpec((1, 1, tn), lambda l, j: (l, 0, j))],
        out_specs=pl.BlockSpec((1, rows, tn), lambda l, j: (l, 0, j)),
        out_shape=jax.ShapeDtypeStruct((DEPTH, rows, width), F32),
        compiler_params=_params(("arbitrary", "arbitrary")),
        name="ada_mod",
    )(c_all, w_ada, b_ada.reshape(DEPTH, 1, width))


def _modulated_norm(x, gain, shift, scale):
    ms = jnp.mean(x * x, axis=-1, keepdims=True)
    y = x * lax.rsqrt(ms + NORM_EPS) * gain
    return y * (1.0 + scale) + shift


def _rotate(x, c, sm, sp, shift):
    return x * c + pltpu.roll(x, LANES - shift, axis=1) * sm + pltpu.roll(x, shift, axis=1) * sp


def _pos_block(i, tm):
    return jnp.where(i < N_PROMPT // tm, i % (SEQ // tm), i % (DEC_SEQ // tm))


def _proj_kernel(x_ref, mod_ref, n1_ref, w_ref, tabd_ref, tabg_ref, gain_ref, grp_ref, o_ref, h_scr, acc_a, acc_b):
    j = pl.program_id(1)
    tile = j - 1
    n_slab = PROJ_TN // LANES
    q_tiles = DIFF_HEADS * 2 * DIFF_HEAD_DIM // PROJ_TN

    def matmul(dst):
        dst[...] = jnp.dot(h_scr[...], w_ref[...], preferred_element_type=F32)

    def rope_epilogue(src):
        c, sm, sp = tabd_ref[0], tabd_ref[1], tabd_ref[2]
        scale = jnp.where(tile < q_tiles, Q_SCALE, 1.0)
        for s in range(n_slab):
            xs = src[:, s * LANES:(s + 1) * LANES] * scale
            o_ref[:, s * LANES:(s + 1) * LANES] = _rotate(xs, c, sm, sp, DIFF_ROT_DIM // 2).astype(BF16)

    def plain_epilogue(src):
        o_ref[...] = src[...].astype(BF16)

    def norm_rope_epilogue(src):
        is_q = tile < COL_GK // PROJ_TN
        gain = gain_ref[pl.ds(jnp.where(is_q, 0, 1), 1), :] * jnp.where(is_q, Q_SCALE, 1.0)
        c, sm, sp = tabg_ref[0], tabg_ref[1], tabg_ref[2]
        for t in range(PROJ_TN // MXU_COLS):
            a = src[:, t * MXU_COLS:(t + 1) * MXU_COLS]
            sq = a * a
            hi = sq.astype(BF16)
            lo = (sq - hi.astype(F32)).astype(BF16)
            ms = (jnp.dot(hi, grp_ref[...], preferred_element_type=F32)
                  + jnp.dot(lo, grp_ref[...], preferred_element_type=F32))
            y = a * lax.rsqrt(ms + NORM_EPS) * gain[:, t * MXU_COLS:(t + 1) * MXU_COLS]
            for s in range(MXU_COLS // LANES):
                col = t * MXU_COLS + s * LANES
                ys = y[:, s * LANES:(s + 1) * LANES]
                o_ref[:, col:col + LANES] = _rotate(ys, c, sm, sp, GQA_HEAD_DIM // 4).astype(BF16)

    def gate_epilogue(src):
        for s in range(n_slab):
            o_ref[:, s * LANES:(s + 1) * LANES] = _sigmoid(src[:, s * LANES:(s + 1) * LANES]).astype(BF16)

    is_plain = ((tile >= COL_DV // PROJ_TN) & (tile < COL_GQ // PROJ_TN)) | (tile == COL_GV // PROJ_TN)
    kinds = ((tile < COL_DV // PROJ_TN, rope_epilogue),
             (is_plain, plain_epilogue),
             ((tile >= COL_GQ // PROJ_TN) & (tile < COL_GV // PROJ_TN), norm_rope_epilogue),
             (tile >= COL_GA // PROJ_TN, gate_epilogue))

    @pl.when(j == 0)
    def _():
        h = _modulated_norm(x_ref[...], n1_ref[...], mod_ref[0, 0:1, :], mod_ref[0, 1:2, :])
        h_scr[...] = h.astype(BF16)
        matmul(acc_a)

    for parity, cur, prev in ((0, acc_a, acc_b), (1, acc_b, acc_a)):
        for cond, epilogue in kinds:
            @pl.when((j > 0) & (j < N_PROJ_TILES) & (j % 2 == parity) & cond)
            def _(cur=cur, prev=prev, epilogue=epilogue):
                matmul(cur)
                epilogue(prev)

    @pl.when(j == N_PROJ_TILES)
    def _():
        gate_epilogue(acc_a if N_PROJ_TILES % 2 else acc_b)


def _proj(x, mod, norm1, w_ext, tab_d, tab_g, gains, grp):
    tm = 1024
    last = N_PROJ_TILES - 1
    return pl.pallas_call(
        _proj_kernel,
        grid=(N_TOK // tm, N_PROJ_TILES + 1),
        in_specs=[pl.BlockSpec((tm, D_MODEL), lambda i, j: (i, 0)),
                  pl.BlockSpec((1, N_MOD, D_MODEL), lambda i, j: (i * tm // MOD_CHUNK, 0, 0)),
                  pl.BlockSpec((1, D_MODEL), lambda i, j: (0, 0)),
                  pl.BlockSpec((D_MODEL, PROJ_TN), lambda i, j: (0, jnp.minimum(j, last))),
                  pl.BlockSpec((3, tm, LANES), lambda i, j: (0, _pos_block(i, tm), 0)),
                  pl.BlockSpec((3, tm, LANES), lambda i, j: (0, _pos_block(i, tm), 0)),
                  pl.BlockSpec((2, PROJ_TN), lambda i, j: (0, 0)),
                  pl.BlockSpec((MXU_COLS, MXU_COLS), lambda i, j: (0, 0))],
        out_specs=pl.BlockSpec((tm, PROJ_TN), lambda i, j: (i, jnp.maximum(j - 1, 0))),
        out_shape=jax.ShapeDtypeStruct((N_TOK, PROJ_W), BF16),
        scratch_shapes=[pltpu.VMEM((tm, D_MODEL), BF16), pltpu.VMEM((tm, PROJ_TN), F32),
                        pltpu.VMEM((tm, PROJ_TN), F32)],
        compiler_params=_params(("arbitrary", "arbitrary")),
        name="proj_in",
    )(x, mod, norm1, w_ext, tab_d, tab_g, gains, grp)


ATT_SCORE_BYTES = 8 * 1024 * 1024


def _attn_kernel(*refs, diff, lam_init, nq, aliased):
    q_ref, k_ref, v_ref = refs[:3]
    if diff:
        lam_ref, subln_ref = refs[3:5]
    n_in = 3 + (2 if diff else 0) + (1 if aliased else 0)
    o_ref, bufs, vx_ref = refs[n_in], refs[n_in + 1:-1], refs[-1]
    set_a, set_b = bufs[:4], bufs[4:]
    half = LANES // 2
    step = pl.program_id(0)

    @pl.when(step == 0)
    def _():
        for ref in set_b:
            ref[...] = jnp.zeros_like(ref)

    @pl.when(jnp.maximum(step - 1, 0) % nq == 0)
    def _():
        v = v_ref[...]
        if diff:
            vx_ref[:, :LANES] = v
            vx_ref[:, LANES:] = jnp.ones_like(v)
        else:
            lane = lax.broadcasted_iota(jnp.int32, v.shape, 1)
            vx_ref[...] = jnp.where(lane < half, v, jnp.ones_like(v))

    def finish(s_ref, m_ref):
        p = jnp.exp2((s_ref[...] - m_ref[...]).astype(BF16))
        acc = jnp.dot(p, vx_ref[...], preferred_element_type=F32)
        return acc[:, :LANES] / acc[:, LANES:] if diff else acc

    def start(qh, s_ref, m_ref):
        s = lax.dot_general(qh, k_ref[...], NT_DIMS, preferred_element_type=F32)
        s_ref[...] = s
        m_ref[...] = jnp.max(s, axis=-1, keepdims=True)

    def run(cur, prev):
        q = q_ref[...]
        low = lax.broadcasted_iota(jnp.int32, q.shape, 1) < half
        zero = jnp.zeros_like(q)
        acc_lo = finish(prev[0], prev[2])
        start(jnp.where(low, q, zero), cur[0], cur[2])
        acc_hi = finish(prev[1], prev[3])
        start(jnp.where(low, zero, q), cur[1], cur[3])
        if diff:
            lp = lam_ref[...]
            t1 = jnp.sum(lp[0:1] * lp[1:2], axis=-1, keepdims=True)
            t2 = jnp.sum(lp[2:3] * lp[3:4], axis=-1, keepdims=True)
            lam = jnp.exp(t1) - jnp.exp(t2) + lam_init
            a = acc_lo - lam * acc_hi
            ms = jnp.mean(a * a, axis=-1, keepdims=True)
            a = a * lax.rsqrt(ms + NORM_EPS) * subln_ref[...] * (1.0 - lam_init)
            o_ref[...] = a.astype(BF16)
        else:
            out = jnp.where(low, acc_lo / pltpu.roll(acc_lo, half, axis=1),
                            pltpu.roll(acc_hi, half, axis=1) / acc_hi)
            o_ref[...] = out.astype(BF16)

    @pl.when(step % 2 == 0)
    def _():
        run(set_a, set_b)

    @pl.when(step % 2 == 1)
    def _():
        run(set_b, set_a)


def _attention(proj, *, diff, batch, seq, row0, lam_init=0.0, lam=None, subln=None, out=None):
    tq = ATT_SCORE_BYTES // (4 * seq)
    nq = seq // tq
    q0 = row0 // tq
    s0 = row0 // seq
    n_units = DIFF_HEADS
    n_blocks = batch * n_units * nq
    kv_unit = (lambda h: h) if diff else (lambda h: h // 2)
    qc, kc, vc = ((COL_DQ, COL_DK, COL_DV) if diff else (COL_GQ, COL_GK, COL_GV))

    def block(g):
        return g // (n_units * nq), (g // nq) % n_units, g % nq

    def started(g):
        return block(jnp.minimum(g, n_blocks - 1))

    def finished(g):
        return block(jnp.maximum(g - 1, 0))

    def q_map(g):
        b, h, i = started(g)
        return q0 + b * nq + i, qc // LANES + h

    def k_map(g):
        b, h, _ = started(g)
        return s0 + b, kc // LANES + kv_unit(h)

    def v_map(g):
        b, h, _ = finished(g)
        return s0 + b, vc // LANES + kv_unit(h)

    def o_map(g):
        b, h, i = finished(g)
        return q0 + b * nq + i, h

    in_specs = [pl.BlockSpec((tq, LANES), q_map),
                pl.BlockSpec((seq, LANES), k_map),
                pl.BlockSpec((seq, LANES), v_map)]
    args = [proj, proj, proj]
    if diff:
        in_specs += [pl.BlockSpec((4, DIFF_HEAD_DIM), lambda g: (0, 0)),
                     pl.BlockSpec((1, LANES), lambda g: (0, 0))]
        args += [lam, subln]
    aliases = {}
    if out is not None:
        aliases = {len(args): 0}
        in_specs.append(pl.BlockSpec(memory_space=pl.ANY))
        args.append(out)
    return pl.pallas_call(
        functools.partial(_attn_kernel, diff=diff, lam_init=lam_init, nq=nq, aliased=out is not None),
        grid=(n_blocks + 1,),
        in_specs=in_specs,
        out_specs=pl.BlockSpec((tq, LANES), o_map),
        out_shape=jax.ShapeDtypeStruct((N_TOK, D_MODEL), BF16),
        input_output_aliases=aliases,
        scratch_shapes=(([pltpu.VMEM((tq, seq), F32)] * 2 + [pltpu.VMEM((tq, 1), F32)] * 2) * 2
                        + [pltpu.VMEM((seq, 2 * LANES if diff else LANES), BF16)]),
        compiler_params=_params(("arbitrary",)),
        name=("diff_attn" if diff else "gqa_attn") + f"_{seq}",
    )(*args)


def _merge_kernel(ga_ref, gb_ref, a_ref, b_ref, x_ref, mod_ref, w_ref, o_ref):
    merged = (ga_ref[...].astype(F32) * a_ref[...].astype(F32)
              + gb_ref[...].astype(F32) * b_ref[...].astype(F32))
    y = jnp.dot(merged.astype(BF16), w_ref[...], preferred_element_type=F32)
    o_ref[...] = x_ref[...] + mod_ref[0, 2:3, :] * y


def _merge(proj, a, b, x, mod, w_out):
    tm = 512
    row = lambda i: (i, 0)
    return pl.pallas_call(
        _merge_kernel,
        grid=(N_TOK // tm,),
        in_specs=[pl.BlockSpec((tm, D_MODEL), lambda i: (i, COL_GA // D_MODEL)),
                  pl.BlockSpec((tm, D_MODEL), lambda i: (i, COL_GB // D_MODEL)),
                  pl.BlockSpec((tm, D_MODEL), row),
                  pl.BlockSpec((tm, D_MODEL), row),
                  pl.BlockSpec((tm, D_MODEL), row),
                  pl.BlockSpec((1, N_MOD, D_MODEL), lambda i: (i * tm // MOD_CHUNK, 0, 0)),
                  pl.BlockSpec((D_MODEL, D_MODEL), lambda i: (0, 0))],
        out_specs=pl.BlockSpec((tm, D_MODEL), row),
        out_shape=jax.ShapeDtypeStruct((N_TOK, D_MODEL), F32),
        compiler_params=_params(("arbitrary",)),
        name="merge_out",
    )(proj, proj, a, b, x, mod, w_out)


def _ffn_kernel(x_ref, mod_ref, n2_ref, wg_ref, wu_ref, wd_ref, o_ref, h_scr):
    f = pl.program_id(1)

    @pl.when(f == 0)
    def _():
        h = _modulated_norm(x_ref[...], n2_ref[...], mod_ref[0, 3:4, :], mod_ref[0, 4:5, :])
        h_scr[...] = h.astype(BF16)
        o_ref[...] = jnp.zeros_like(o_ref)

    h = h_scr[...]
    g = jnp.dot(h, wg_ref[...], preferred_element_type=F32)
    u = jnp.dot(h, wu_ref[...], preferred_element_type=F32)
    act = (g * _sigmoid(g) * u).astype(BF16)
    o_ref[...] += jnp.dot(act, wd_ref[...], preferred_element_type=F32)

    @pl.when(f == pl.num_programs(1) - 1)
    def _():
        o_ref[...] = x_ref[...] + mod_ref[0, 5:6, :] * o_ref[...]


def _ffn(x, mod, norm2, wg, wu, wd):
    tm, tf = 1024, 256
    return pl.pallas_call(
        _ffn_kernel,
        grid=(N_TOK // tm, D_FF_DENSE // tf),
        in_specs=[pl.BlockSpec((tm, D_MODEL), lambda i, f: (i, 0)),
                  pl.BlockSpec((1, N_MOD, D_MODEL), lambda i, f: (i * tm // MOD_CHUNK, 0, 0)),
                  pl.BlockSpec((1, D_MODEL), lambda i, f: (0, 0)),
                  pl.BlockSpec((D_MODEL, tf), lambda i, f: (0, f)),
                  pl.BlockSpec((D_MODEL, tf), lambda i, f: (0, f)),
                  pl.BlockSpec((tf, D_MODEL), lambda i, f: (f, 0))],
        out_specs=pl.BlockSpec((tm, D_MODEL), lambda i, f: (i, 0)),
        out_shape=jax.ShapeDtypeStruct((N_TOK, D_MODEL), F32),
        scratch_shapes=[pltpu.VMEM((tm, D_MODEL), BF16)],
        compiler_params=_params(("arbitrary", "arbitrary")),
        name="ffn_dense",
    )(x, mod, norm2, wg, wu, wd)


MOE_TM = 3 * MXU_COLS


def _tile_mod(mod_a_ref, mod_b_ref, row, tm, mod_chunk):
    i = pl.program_id(0)
    first_of_b = ((i + 1) * tm - 1) // mod_chunk * mod_chunk
    token = i * tm + lax.broadcasted_iota(jnp.int32, (tm, 1), 0)
    return jnp.where(token >= first_of_b, mod_b_ref[0, row:row + 1, :], mod_a_ref[0, row:row + 1, :])


def _router_kernel(x_ref, mod_a_ref, mod_b_ref, n2_ref, rw_ref, ht_ref, gate_ref, rank_ref, rankt_ref, cnt_ref,
                   *, mod_chunk):
    tm = x_ref.shape[0]
    h = _modulated_norm(x_ref[...], n2_ref[...], _tile_mod(mod_a_ref, mod_b_ref, 3, tm, mod_chunk),
                        _tile_mod(mod_a_ref, mod_b_ref, 4, tm, mod_chunk))
    ht_ref[...] = h.T.astype(BF16)
    logits = jnp.dot(h, rw_ref[...], preferred_element_type=F32, precision=lax.Precision.HIGHEST)
    lane = lax.broadcasted_iota(jnp.int32, logits.shape, 1).astype(F32)
    neg = jnp.float32(-jnp.inf)
    lg = jnp.where(lane < N_EXPERTS, logits, neg)
    m1 = jnp.max(lg, axis=-1, keepdims=True)
    i1 = jnp.min(jnp.where(lg == m1, lane, float(LANES)), axis=-1, keepdims=True)
    lg2 = jnp.where(lane == i1, neg, lg)
    m2 = jnp.max(lg2, axis=-1, keepdims=True)
    i2 = jnp.min(jnp.where(lg2 == m2, lane, float(LANES)), axis=-1, keepdims=True)
    e2 = jnp.exp(m2 - m1)
    den = 1.0 + e2
    gate_ref[...] = jnp.where(lane == i1, 1.0 / den, 0.0) + jnp.where(lane == i2, e2 / den, 0.0)
    sel = (lane == i1) | (lane == i2)
    sel_f = jnp.where(sel, 1.0, 0.0)
    before = (lax.broadcasted_iota(jnp.int32, (tm, tm), 1)
              < lax.broadcasted_iota(jnp.int32, (tm, tm), 0))
    slot = jnp.dot(jnp.where(before, 1.0, 0.0).astype(BF16), sel_f.astype(BF16), preferred_element_type=F32)
    rank = jnp.where(sel, slot, -1.0)
    rank_ref[...] = rank
    rankt_ref[...] = rank.T[:SUBLANES, :]
    cnt = jnp.sum(sel_f, axis=0, keepdims=True)
    cnt_ref[...] = jnp.broadcast_to(cnt, (SUBLANES, LANES)).astype(jnp.int32)


def _router(x, mod, norm2, rw, *, tm=MOE_TM, mod_chunk=MOD_CHUNK):
    n, d = x.shape
    nt = n // tm
    return pl.pallas_call(
        functools.partial(_router_kernel, mod_chunk=mod_chunk),
        grid=(nt,),
        in_specs=[pl.BlockSpec((tm, d), lambda i: (i, 0)),
                  pl.BlockSpec((1, N_MOD, d), lambda i: (i * tm // mod_chunk, 0, 0)),
                  pl.BlockSpec((1, N_MOD, d), lambda i: (((i + 1) * tm - 1) // mod_chunk, 0, 0)),
                  pl.BlockSpec((1, d), lambda i: (0, 0)),
                  pl.BlockSpec((d, LANES), lambda i: (0, 0))],
        out_specs=[pl.BlockSpec((d, tm), lambda i: (0, i)),
                   pl.BlockSpec((tm, LANES), lambda i: (i, 0)),
                   pl.BlockSpec((tm, LANES), lambda i: (i, 0)),
                   pl.BlockSpec((SUBLANES, tm), lambda i: (i, 0)),
                   pl.BlockSpec((SUBLANES, LANES), lambda i: (i, 0))],
        out_shape=[jax.ShapeDtypeStruct((d, n), BF16),
                   jax.ShapeDtypeStruct((n, LANES), F32),
                   jax.ShapeDtypeStruct((n, LANES), F32),
                   jax.ShapeDtypeStruct((nt * SUBLANES, tm), F32),
                   jax.ShapeDtypeStruct((nt * SUBLANES, LANES), jnp.int32)],
        compiler_params=_params(("arbitrary",)),
        name="moe_router",
    )(x, mod, mod, norm2, rw)


def _moe_kernel(cnt_ref, ht_ref, rankt_ref, rank_ref, gate_ref, x_ref, mod_a_ref, mod_b_ref, wg_ref, wu_ref, wd_ref,
                o_ref, xt_scr, yt_scr, *, mod_chunk):
    i, e, f = pl.program_id(0), pl.program_id(1), pl.program_id(2)
    last_f = pl.num_programs(2) - 1
    tm = o_ref.shape[0]
    n_blk = (cnt_ref[i * N_EXPERTS + e] + MXU_COLS - 1) // MXU_COLS

    @pl.when((e == 0) & (f == 0))
    def _():
        o_ref[...] = jnp.zeros_like(o_ref)

    @pl.when(f == 0)
    def _():
        slot = rankt_ref[pl.ds(e, 1), :].astype(jnp.int32)
        sub = lax.broadcasted_iota(jnp.int32, (MXU_COLS, tm), 0)

        def gather(b, carry):
            pick = jnp.where(sub + b * MXU_COLS == slot, 1.0, 0.0).astype(BF16)
            xt_scr[b] = lax.dot_general(ht_ref[...], pick, NT_DIMS, preferred_element_type=F32).astype(BF16)
            yt_scr[b] = jnp.zeros(yt_scr.shape[1:], F32)
            return carry

        lax.fori_loop(0, n_blk, gather, 0)

    def expert(b, carry):
        xb = xt_scr[b]
        g = jnp.dot(wg_ref[0], xb, preferred_element_type=F32)
        u = jnp.dot(wu_ref[0], xb, preferred_element_type=F32)
        act = (g * _sigmoid(g) * u).astype(BF16)
        yt_scr[b] += jnp.dot(wd_ref[0], act, preferred_element_type=F32)
        return carry

    lax.fori_loop(0, n_blk, expert, 0)

    @pl.when(f == last_f)
    def _():
        on_lane = lax.broadcasted_iota(jnp.int32, (tm, LANES), 1) == e
        slot = jnp.sum(jnp.where(on_lane, rank_ref[...], 0.0), axis=-1, keepdims=True).astype(jnp.int32)
        gate = jnp.sum(jnp.where(on_lane, gate_ref[...], 0.0), axis=-1, keepdims=True)
        col = lax.broadcasted_iota(jnp.int32, (tm, MXU_COLS), 1)

        def scatter(b, carry):
            pick = jnp.where(col + b * MXU_COLS == slot, 1.0, 0.0).astype(BF16)
            y = lax.dot_general(pick, yt_scr[b].astype(BF16), NT_DIMS, preferred_element_type=F32)
            o_ref[...] += gate * y
            return carry

        lax.fori_loop(0, n_blk, scatter, 0)

    @pl.when((e == pl.num_programs(1) - 1) & (f == last_f))
    def _():
        o_ref[...] = x_ref[...] + _tile_mod(mod_a_ref, mod_b_ref, 5, tm, mod_chunk) * o_ref[...]


def _moe(counts, ht, rankt, rank, gate, x, mod, wgt, wut, wdt, *, tm=MOE_TM, tf=896, mod_chunk=MOD_CHUNK):
    n, d = x.shape
    n_exp, d_ff, _ = wgt.shape
    grid_spec = pltpu.PrefetchScalarGridSpec(
        num_scalar_prefetch=1,
        grid=(n // tm, n_exp, d_ff // tf),
        in_specs=[pl.BlockSpec((d, tm), lambda i, e, f, c: (0, i)),
                  pl.BlockSpec((SUBLANES, tm), lambda i, e, f, c: (i, 0)),
                  pl.BlockSpec((tm, LANES), lambda i, e, f, c: (i, 0)),
                  pl.BlockSpec((tm, LANES), lambda i, e, f, c: (i, 0)),
                  pl.BlockSpec((tm, d), lambda i, e, f, c: (i, 0)),
                  pl.BlockSpec((1, N_MOD, d), lambda i, e, f, c: (i * tm // mod_chunk, 0, 0)),
                  pl.BlockSpec((1, N_MOD, d), lambda i, e, f, c: (((i + 1) * tm - 1) // mod_chunk, 0, 0)),
                  pl.BlockSpec((1, tf, d), lambda i, e, f, c: (e, f, 0)),
                  pl.BlockSpec((1, tf, d), lambda i, e, f, c: (e, f, 0)),
                  pl.BlockSpec((1, d, tf), lambda i, e, f, c: (e, 0, f))],
        out_specs=pl.BlockSpec((tm, d), lambda i, e, f, c: (i, 0)),
        scratch_shapes=[pltpu.VMEM((tm // MXU_COLS, d, MXU_COLS), BF16),
                        pltpu.VMEM((tm // MXU_COLS, d, MXU_COLS), F32)])
    return pl.pallas_call(
        functools.partial(_moe_kernel, mod_chunk=mod_chunk),
        grid_spec=grid_spec,
        out_shape=jax.ShapeDtypeStruct((n, d), F32),
        compiler_params=_params(("arbitrary", "arbitrary", "arbitrary")),
        name="moe_experts",
    )(counts, ht, rankt, rank, gate, x, mod, mod, wgt, wut, wdt)


def _moe_layer(x, mod, norm2, router_w, w_gate, w_up, w_down):
    rw = jnp.pad(router_w, ((0, 0), (0, LANES - N_EXPERTS)))
    ht, gate, rank, rankt, cnt = _router(x, mod, norm2, rw)
    counts = cnt.reshape(-1, SUBLANES, LANES)[:, 0, :N_EXPERTS].reshape(-1)
    wgt = jnp.swapaxes(w_gate.astype(BF16), 1, 2)
    wut = jnp.swapaxes(w_up.astype(BF16), 1, 2)
    wdt = jnp.swapaxes(w_down.astype(BF16), 1, 2)
    return _moe(counts, ht, rankt, rank, gate, x, mod, wgt, wut, wdt)


def _final_kernel(x_ref, g_ref, prompt_ref, sample_ref, *, prompt_tiles):
    x = x_ref[...]
    ms = jnp.mean(x * x, axis=-1, keepdims=True)
    y = x * lax.rsqrt(ms + NORM_EPS) * g_ref[...]
    i = pl.program_id(0)

    @pl.when(i < prompt_tiles)
    def _():
        prompt_ref[...] = y

    @pl.when(i >= prompt_tiles)
    def _():
        sample_ref[...] = y


def _final_norm(x, gain):
    tm = 1024
    prompt_tiles = N_PROMPT // tm
    return pl.pallas_call(
        functools.partial(_final_kernel, prompt_tiles=prompt_tiles),
        grid=(N_TOK // tm,),
        in_specs=[pl.BlockSpec((tm, D_MODEL), lambda i: (i, 0)),
                  pl.BlockSpec((1, D_MODEL), lambda i: (0, 0))],
        out_specs=[pl.BlockSpec((tm, D_MODEL), lambda i: (jnp.minimum(i, prompt_tiles - 1), 0)),
                   pl.BlockSpec((tm, D_MODEL), lambda i: (jnp.maximum(i - prompt_tiles, 0), 0))],
        out_shape=[jax.ShapeDtypeStruct((N_PROMPT, D_MODEL), F32),
                   jax.ShapeDtypeStruct((N_SAMPLE, D_MODEL), F32)],
        compiler_params=_params(("arbitrary",)),
        name="final_norm",
    )(x, gain)


def _rope_tables():
    pos = jnp.arange(DEC_SEQ, dtype=jnp.int32)
    d = jnp.arange(LANES, dtype=jnp.int32) % DIFF_HEAD_DIM

    def angles(p, theta, idx, half):
        inv_freq = jnp.exp(-math.log(theta) * jnp.arange(half, dtype=F32) / half)
        return p.astype(F32)[:, None] * inv_freq[idx][None, :]

    half_d = DIFF_ROT_DIM // 2
    ang = angles(pos, ROPE_THETA, d % half_d, half_d)
    active = (d < DIFF_ROT_DIM)[None, :]
    first = (d < half_d)[None, :]
    tab_d = jnp.stack([jnp.where(active, jnp.cos(ang), 1.0),
                       jnp.where(first, -jnp.sin(ang), 0.0),
                       jnp.where(active & ~first, jnp.sin(ang), 0.0)])

    half_g = GQA_HEAD_DIM // 4
    use_row = (d < GQA_HEAD_DIM // 2)[None, :]
    ang_r = angles(pos // GRID_W, AXIAL_THETA, d % half_g, half_g)
    ang_c = angles(pos % GRID_W, AXIAL_THETA, d % half_g, half_g)
    ang = jnp.where(use_row, ang_r, ang_c)
    first = ((d % (2 * half_g)) < half_g)[None, :]
    tab_g = jnp.stack([jnp.cos(ang),
                       jnp.where(first, -jnp.sin(ang), 0.0),
                       jnp.where(first, 0.0, jnp.sin(ang))])
    return tab_d.astype(F32), tab_g.astype(F32)


def _extend_w_in(w):
    def dup(cols):
        c = cols.reshape(D_MODEL, GQA_KV_HEADS, 1, GQA_HEAD_DIM)
        return jnp.broadcast_to(c, (D_MODEL, GQA_KV_HEADS, 2, GQA_HEAD_DIM)).reshape(D_MODEL, -1)
    kv = GQA_KV_HEADS * GQA_HEAD_DIM
    gk = w[:, COL_GK:COL_GK + kv]
    gv = w[:, COL_GK + kv:COL_GK + 2 * kv]
    return jnp.concatenate([w[:, :COL_GK], dup(gk), dup(gv), w[:, COL_GK + 2 * kv:]], axis=1)


def kernel(x_prompt, x_sample, c_prompt, c_sample, w_ada, b_ada, norm1, w_in, diff_lambda, diff_subln,
           gqa_q_norm, gqa_k_norm, w_out, norm2, ffn_w_gate, ffn_w_up, ffn_w_down, router_w,
           moe_w_gate, moe_w_up, moe_w_down, final_norm):
    x = jnp.concatenate([x_prompt.reshape(N_PROMPT, D_MODEL), x_sample.reshape(N_SAMPLE, D_MODEL)], axis=0)
    n_c = BATCH + DEC_BATCH
    c_all = jnp.concatenate([c_prompt, c_sample, jnp.zeros((16 - n_c, D_MODEL), F32)], axis=0)
    mod_all = _ada(c_all, w_ada, b_ada)
    chunk_src = jnp.asarray(list(range(BATCH)) + [BATCH + i // 2 for i in range(2 * DEC_BATCH)], jnp.int32)
    tab_d, tab_g = _rope_tables()
    grp = jnp.kron(jnp.eye(MXU_COLS // GQA_HEAD_DIM, dtype=F32),
                   jnp.full((GQA_HEAD_DIM, GQA_HEAD_DIM), 1.0 / GQA_HEAD_DIM, F32)).astype(BF16)

    for l in range(DEPTH):
        mod = mod_all[l][chunk_src].reshape(N_CHUNKS, N_MOD, D_MODEL)
        lam_init = 0.8 - 0.6 * math.exp(-0.3 * l)
        w_ext = _extend_w_in(w_in[l]).astype(BF16)
        gains = jnp.stack([jnp.tile(gqa_q_norm[l], PROJ_TN // GQA_HEAD_DIM),
                           jnp.tile(gqa_k_norm[l], PROJ_TN // GQA_HEAD_DIM)])
        proj = _proj(x, mod, norm1[l][None, :], w_ext, tab_d, tab_g, gains, grp)
        subln = diff_subln[l][None, :]
        a, b = None, None
        for batch, seq, row0 in ((BATCH, SEQ, 0), (DEC_BATCH, DEC_SEQ, N_PROMPT)):
            a = _attention(proj, diff=True, batch=batch, seq=seq, row0=row0, lam_init=lam_init,
                           lam=diff_lambda[l], subln=subln, out=a)
            b = _attention(proj, diff=False, batch=batch, seq=seq, row0=row0, out=b)
        x = _merge(proj, a, b, x, mod, w_out[l].astype(BF16))
        if l % 2 == 0:
            x = _ffn(x, mod, norm2[l][None, :], ffn_w_gate[l // 2].astype(BF16),
                     ffn_w_up[l // 2].astype(BF16), ffn_w_down[l // 2].astype(BF16))
        else:
            x = _moe_layer(x, mod, norm2[l][None, :], router_w[l // 2], moe_w_gate[l // 2],
                           moe_w_up[l // 2], moe_w_down[l // 2])
    y_prompt, y_sample = _final_norm(x, final_norm[None, :])
    return (y_prompt.reshape(BATCH, SEQ, D_MODEL), y_sample.reshape(DEC_BATCH, DEC_SEQ, D_MODEL))
```

```python
import functools
import math

import jax
import jax.numpy as jnp
from jax import lax
from jax.experimental import pallas as pl
from jax.experimental.pallas import tpu as pltpu

F32 = jnp.float32
BF16 = jnp.bfloat16

D_MODEL = 1024
BATCH = 8
SEQ = 4096
DEPTH = 4
DEC_BATCH = 2
DEC_SEQ = 8192
GRID_W = 64
DIFF_HEADS = 8
DIFF_HEAD_DIM = 64
DIFF_ROT_DIM = DIFF_HEAD_DIM // 4
ROPE_THETA = 500000.0
GQA_Q_HEADS = 16
GQA_KV_HEADS = 4
GQA_HEAD_DIM = 64
AXIAL_THETA = 10000.0
D_FF_DENSE = 2816
N_EXPERTS = 8
D_FF_EXPERT = 3584
NORM_EPS = 1e-6
N_MOD = 6

N_PROMPT = BATCH * SEQ
N_SAMPLE = DEC_BATCH * DEC_SEQ
N_TOK = N_PROMPT + N_SAMPLE
MOD_CHUNK = SEQ
N_CHUNKS = N_TOK // MOD_CHUNK

LANES = 128
SUBLANES = 8
MXU_COLS = 256
VMEM_LIMIT = 56 * 1024 * 1024

Q_SCALE = DIFF_HEAD_DIM ** -0.5 * math.log2(math.e)

PROJ_TN = 512
COL_DQ, COL_DK, COL_DV, COL_GQ = 0, 1024, 2048, 3072
COL_GK, COL_GV, COL_GA, COL_GB = 4096, 4608, 5120, 6144
PROJ_W = 7168
N_PROJ_TILES = PROJ_W // PROJ_TN

NT_DIMS = (((1,), (1,)), ((), ()))


def _params(sem, vmem=VMEM_LIMIT):
    return pltpu.CompilerParams(dimension_semantics=sem, vmem_limit_bytes=vmem)


def _sigmoid(x):
    return 0.5 * jnp.tanh(0.5 * x) + 0.5


def _ada_kernel(c_ref, w_ref, b_ref, o_ref):
    c = c_ref[...]
    ca = c * _sigmoid(c)
    o_ref[0] = jnp.dot(ca.astype(BF16), w_ref[0].astype(BF16), preferred_element_type=F32) + b_ref[0]


def _ada(c_all, w_ada, b_ada):
    rows = c_all.shape[0]
    tn = 1536
    width = N_MOD * D_MODEL
    return pl.pallas_call(
        _ada_kernel,
        grid=(DEPTH, width // tn),
        in_specs=[pl.BlockSpec((rows, D_MODEL), lambda l, j: (0, 0)),
                  pl.BlockSpec((1, D_MODEL, tn), lambda l, j: (l, 0, j)),
                  pl.BlockSpec((1, 1, tn), lambda l, j: (l, 0, j))],
        out_specs=pl.BlockSpec((1, rows, tn), lambda l, j: (l, 0, j)),
        out_shape=jax.ShapeDtypeStruct((DEPTH, rows, width), F32),
        compiler_params=_params(("arbitrary", "arbitrary")),
        name="ada_mod",
    )(c_all, w_ada, b_ada.reshape(DEPTH, 1, width))


def _modulated_norm(x, gain, shift, scale):
    ms = jnp.mean(x * x, axis=-1, keepdims=True)
    y = x * lax.rsqrt(ms + NORM_EPS) * gain
    return y * (1.0 + scale) + shift


def _rotate(x, c, sm, sp, shift):
    return x * c + pltpu.roll(x, LANES - shift, axis=1) * sm + pltpu.roll(x, shift, axis=1) * sp


def _pos_block(i, tm):
    return jnp.where(i < N_PROMPT // tm, i % (SEQ // tm), i % (DEC_SEQ // tm))


def _proj_kernel(x_ref, mod_ref, n1_ref, w_ref, tabd_ref, tabg_ref, gain_ref, grp_ref, o_ref, h_scr, acc_a, acc_b):
    j = pl.program_id(1)
    tile = j - 1
    n_slab = PROJ_TN // LANES
    q_tiles = DIFF_HEADS * 2 * DIFF_HEAD_DIM // PROJ_TN

    def matmul(dst):
        dst[...] = jnp.dot(h_scr[...], w_ref[...], preferred_element_type=F32)

    def rope_epilogue(src):
        c, sm, sp = tabd_ref[0], tabd_ref[1], tabd_ref[2]
        scale = jnp.where(tile < q_tiles, Q_SCALE, 1.0)
        for s in range(n_slab):
            xs = src[:, s * LANES:(s + 1) * LANES] * scale
            o_ref[:, s * LANES:(s + 1) * LANES] = _rotate(xs, c, sm, sp, DIFF_ROT_DIM // 2).astype(BF16)

    def plain_epilogue(src):
        o_ref[...] = src[...].astype(BF16)

    def norm_rope_epilogue(src):
        is_q = tile < COL_GK // PROJ_TN
        gain = gain_ref[pl.ds(jnp.where(is_q, 0, 1), 1), :] * jnp.where(is_q, Q_SCALE, 1.0)
        c, sm, sp = tabg_ref[0], tabg_ref[1], tabg_ref[2]
        for t in range(PROJ_TN // MXU_COLS):
            a = src[:, t * MXU_COLS:(t + 1) * MXU_COLS]
            sq = a * a
            hi = sq.astype(BF16)
            lo = (sq - hi.astype(F32)).astype(BF16)
            ms = (jnp.dot(hi, grp_ref[...], preferred_element_type=F32)
                  + jnp.dot(lo, grp_ref[...], preferred_element_type=F32))
            y = a * lax.rsqrt(ms + NORM_EPS) * gain[:, t * MXU_COLS:(t + 1) * MXU_COLS]
            for s in range(MXU_COLS // LANES):
                col = t * MXU_COLS + s * LANES
                ys = y[:, s * LANES:(s + 1) * LANES]
                o_ref[:, col:col + LANES] = _rotate(ys, c, sm, sp, GQA_HEAD_DIM // 4).astype(BF16)

    def gate_epilogue(src):
        for s in range(n_slab):
            o_ref[:, s * LANES:(s + 1) * LANES] = _sigmoid(src[:, s * LANES:(s + 1) * LANES]).astype(BF16)

    is_plain = ((tile >= COL_DV // PROJ_TN) & (tile < COL_GQ // PROJ_TN)) | (tile == COL_GV // PROJ_TN)
    kinds = ((tile < COL_DV // PROJ_TN, rope_epilogue),
             (is_plain, plain_epilogue),
             ((tile >= COL_GQ // PROJ_TN) & (tile < COL_GV // PROJ_TN), norm_rope_epilogue),
             (tile >= COL_GA // PROJ_TN, gate_epilogue))

    @pl.when(j == 0)
    def _():
        h = _modulated_norm(x_ref[...], n1_ref[...], mod_ref[0, 0:1, :], mod_ref[0, 1:2, :])
        h_scr[...] = h.astype(BF16)
        matmul(acc_a)

    for parity, cur, prev in ((0, acc_a, acc_b), (1, acc_b, acc_a)):
        for cond, epilogue in kinds:
            @pl.when((j > 0) & (j < N_PROJ_TILES) & (j % 2 == parity) & cond)
            def _(cur=cur, prev=prev, epilogue=epilogue):
                matmul(cur)
                epilogue(prev)

    @pl.when(j == N_PROJ_TILES)
    def _():
        gate_epilogue(acc_a if N_PROJ_TILES % 2 else acc_b)


def _proj(x, mod, norm1, w_ext, tab_d, tab_g, gains, grp):
    tm = 1024
    last = N_PROJ_TILES - 1
    return pl.pallas_call(
        _proj_kernel,
        grid=(N_TOK // tm, N_PROJ_TILES + 1),
        in_specs=[pl.BlockSpec((tm, D_MODEL), lambda i, j: (i, 0)),
                  pl.BlockSpec((1, N_MOD, D_MODEL), lambda i, j: (i * tm // MOD_CHUNK, 0, 0)),
                  pl.BlockSpec((1, D_MODEL), lambda i, j: (0, 0)),
                  pl.BlockSpec((D_MODEL, PROJ_TN), lambda i, j: (0, jnp.minimum(j, last))),
                  pl.BlockSpec((3, tm, LANES), lambda i, j: (0, _pos_block(i, tm), 0)),
                  pl.BlockSpec((3, tm, LANES), lambda i, j: (0, _pos_block(i, tm), 0)),
                  pl.BlockSpec((2, PROJ_TN), lambda i, j: (0, 0)),
                  pl.BlockSpec((MXU_COLS, MXU_COLS), lambda i, j: (0, 0))],
        out_specs=pl.BlockSpec((tm, PROJ_TN), lambda i, j: (i, jnp.maximum(j - 1, 0))),
        out_shape=jax.ShapeDtypeStruct((N_TOK, PROJ_W), BF16),
        scratch_shapes=[pltpu.VMEM((tm, D_MODEL), BF16), pltpu.VMEM((tm, PROJ_TN), F32),
                        pltpu.VMEM((tm, PROJ_TN), F32)],
        compiler_params=_params(("arbitrary", "arbitrary")),
        name="proj_in",
    )(x, mod, norm1, w_ext, tab_d, tab_g, gains, grp)


ATT_SCORE_BYTES = 8 * 1024 * 1024


def _attn_kernel(*refs, diff, lam_init, nq, aliased):
    q_ref, k_ref, v_ref = refs[:3]
    if diff:
        lam_ref, subln_ref = refs[3:5]
    n_in = 3 + (2 if diff else 0) + (1 if aliased else 0)
    o_ref, bufs, vx_ref = refs[n_in], refs[n_in + 1:-1], refs[-1]
    set_a, set_b = bufs[:4], bufs[4:]
    half = LANES // 2
    step = pl.program_id(0)

    @pl.when(step == 0)
    def _():
        for ref in set_b:
            ref[...] = jnp.zeros_like(ref)

    @pl.when(jnp.maximum(step - 1, 0) % nq == 0)
    def _():
        v = v_ref[...]
        if diff:
            vx_ref[:, :LANES] = v
            vx_ref[:, LANES:] = jnp.ones_like(v)
        else:
            lane = lax.broadcasted_iota(jnp.int32, v.shape, 1)
            vx_ref[...] = jnp.where(lane < half, v, jnp.ones_like(v))

    def finish(s_ref, m_ref):
        m = jnp.max(m_ref[...], axis=-1, keepdims=True)
        p = jnp.exp2((s_ref[...] - m).astype(BF16))
        acc = jnp.dot(p, vx_ref[...], preferred_element_type=F32)
        return acc[:, :LANES] / acc[:, LANES:] if diff else acc

    def start(qh, s_ref, m_ref):
        s = lax.dot_general(qh, k_ref[...], NT_DIMS, preferred_element_type=F32)
        s_ref[...] = s
        top = s[:, :LANES]
        for u in range(1, s.shape[1] // LANES):
            top = jnp.maximum(top, s[:, u * LANES:(u + 1) * LANES])
        m_ref[...] = top

    def run(cur, prev):
        q = q_ref[...]
        low = lax.broadcasted_iota(jnp.int32, q.shape, 1) < half
        zero = jnp.zeros_like(q)
        acc_lo = finish(prev[0], prev[2])
        start(jnp.where(low, q, zero), cur[0], cur[2])
        acc_hi = finish(prev[1], prev[3])
        if diff:
            lp = lam_ref[...]
            t1 = jnp.sum(lp[0:1] * lp[1:2], axis=-1, keepdims=True)
            t2 = jnp.sum(lp[2:3] * lp[3:4], axis=-1, keepdims=True)
            lam = jnp.exp(t1) - jnp.exp(t2) + lam_init
            a = acc_lo - lam * acc_hi
            ms = jnp.mean(a * a, axis=-1, keepdims=True)
            a = a * lax.rsqrt(ms + NORM_EPS) * subln_ref[...] * (1.0 - lam_init)
            o_ref[...] = a.astype(BF16)
        else:
            out = jnp.where(low, acc_lo / pltpu.roll(acc_lo, half, axis=1),
                            pltpu.roll(acc_hi, half, axis=1) / acc_hi)
            o_ref[...] = out.astype(BF16)
        start(jnp.where(low, zero, q), cur[1], cur[3])

    @pl.when(step % 2 == 0)
    def _():
        run(set_a, set_b)

    @pl.when(step % 2 == 1)
    def _():
        run(set_b, set_a)


def _attention(proj, *, diff, batch, seq, row0, lam_init=0.0, lam=None, subln=None, out=None):
    tq = ATT_SCORE_BYTES // (4 * seq)
    nq = seq // tq
    q0 = row0 // tq
    s0 = row0 // seq
    n_units = DIFF_HEADS
    n_blocks = batch * n_units * nq
    kv_unit = (lambda h: h) if diff else (lambda h: h // 2)
    qc, kc, vc = ((COL_DQ, COL_DK, COL_DV) if diff else (COL_GQ, COL_GK, COL_GV))

    def block(g):
        return g // (n_units * nq), (g // nq) % n_units, g % nq

    def started(g):
        return block(jnp.minimum(g, n_blocks - 1))

    def finished(g):
        return block(jnp.maximum(g - 1, 0))

    def q_map(g):
        b, h, i = started(g)
        return q0 + b * nq + i, qc // LANES + h

    def k_map(g):
        b, h, _ = started(g)
        return s0 + b, kc // LANES + kv_unit(h)

    def v_map(g):
        b, h, _ = finished(g)
        return s0 + b, vc // LANES + kv_unit(h)

    def o_map(g):
        b, h, i = finished(g)
        return q0 + b * nq + i, h

    in_specs = [pl.BlockSpec((tq, LANES), q_map),
                pl.BlockSpec((seq, LANES), k_map),
                pl.BlockSpec((seq, LANES), v_map)]
    args = [proj, proj, proj]
    if diff:
        in_specs += [pl.BlockSpec((4, DIFF_HEAD_DIM), lambda g: (0, 0)),
                     pl.BlockSpec((1, LANES), lambda g: (0, 0))]
        args += [lam, subln]
    aliases = {}
    if out is not None:
        aliases = {len(args): 0}
        in_specs.append(pl.BlockSpec(memory_space=pl.ANY))
        args.append(out)
    return pl.pallas_call(
        functools.partial(_attn_kernel, diff=diff, lam_init=lam_init, nq=nq, aliased=out is not None),
        grid=(n_blocks + 1,),
        in_specs=in_specs,
        out_specs=pl.BlockSpec((tq, LANES), o_map),
        out_shape=jax.ShapeDtypeStruct((N_TOK, D_MODEL), BF16),
        input_output_aliases=aliases,
        scratch_shapes=(([pltpu.VMEM((tq, seq), F32)] * 2 + [pltpu.VMEM((tq, LANES), F32)] * 2) * 2
                        + [pltpu.VMEM((seq, 2 * LANES if diff else LANES), BF16)]),
        compiler_params=_params(("arbitrary",)),
        name=("diff_attn" if diff else "gqa_attn") + f"_{seq}",
    )(*args)


def _merge_kernel(ga_ref, gb_ref, a_ref, b_ref, x_ref, mod_ref, w_ref, o_ref):
    merged = (ga_ref[...].astype(F32) * a_ref[...].astype(F32)
              + gb_ref[...].astype(F32) * b_ref[...].astype(F32))
    y = jnp.dot(merged.astype(BF16), w_ref[...], preferred_element_type=F32)
    o_ref[...] = x_ref[...] + mod_ref[0, 2:3, :] * y


def _merge(proj, a, b, x, mod, w_out):
    tm = 512
    row = lambda i: (i, 0)
    return pl.pallas_call(
        _merge_kernel,
        grid=(N_TOK // tm,),
        in_specs=[pl.BlockSpec((tm, D_MODEL), lambda i: (i, COL_GA // D_MODEL)),
                  pl.BlockSpec((tm, D_MODEL), lambda i: (i, COL_GB // D_MODEL)),
                  pl.BlockSpec((tm, D_MODEL), row),
                  pl.BlockSpec((tm, D_MODEL), row),
                  pl.BlockSpec((tm, D_MODEL), row),
                  pl.BlockSpec((1, N_MOD, D_MODEL), lambda i: (i * tm // MOD_CHUNK, 0, 0)),
                  pl.BlockSpec((D_MODEL, D_MODEL), lambda i: (0, 0))],
        out_specs=pl.BlockSpec((tm, D_MODEL), row),
        out_shape=jax.ShapeDtypeStruct((N_TOK, D_MODEL), F32),
        compiler_params=_params(("arbitrary",)),
        name="merge_out",
    )(proj, proj, a, b, x, mod, w_out)


def _ffn_kernel(x_ref, mod_ref, n2_ref, wg_ref, wu_ref, wd_ref, o_ref, h_scr):
    f = pl.program_id(1)

    @pl.when(f == 0)
    def _():
        h = _modulated_norm(x_ref[...], n2_ref[...], mod_ref[0, 3:4, :], mod_ref[0, 4:5, :])
        h_scr[...] = h.astype(BF16)
        o_ref[...] = jnp.zeros_like(o_ref)

    h = h_scr[...]
    g = jnp.dot(h, wg_ref[...], preferred_element_type=F32)
    u = jnp.dot(h, wu_ref[...], preferred_element_type=F32)
    act = (g * _sigmoid(g) * u).astype(BF16)
    o_ref[...] += jnp.dot(act, wd_ref[...], preferred_element_type=F32)

    @pl.when(f == pl.num_programs(1) - 1)
    def _():
        o_ref[...] = x_ref[...] + mod_ref[0, 5:6, :] * o_ref[...]


def _ffn(x, mod, norm2, wg, wu, wd):
    tm, tf = 1024, 256
    return pl.pallas_call(
        _ffn_kernel,
        grid=(N_TOK // tm, D_FF_DENSE // tf),
        in_specs=[pl.BlockSpec((tm, D_MODEL), lambda i, f: (i, 0)),
                  pl.BlockSpec((1, N_MOD, D_MODEL), lambda i, f: (i * tm // MOD_CHUNK, 0, 0)),
                  pl.BlockSpec((1, D_MODEL), lambda i, f: (0, 0)),
                  pl.BlockSpec((D_MODEL, tf), lambda i, f: (0, f)),
                  pl.BlockSpec((D_MODEL, tf), lambda i, f: (0, f)),
                  pl.BlockSpec((tf, D_MODEL), lambda i, f: (f, 0))],
        out_specs=pl.BlockSpec((tm, D_MODEL), lambda i, f: (i, 0)),
        out_shape=jax.ShapeDtypeStruct((N_TOK, D_MODEL), F32),
        scratch_shapes=[pltpu.VMEM((tm, D_MODEL), BF16)],
        compiler_params=_params(("arbitrary", "arbitrary")),
        name="ffn_dense",
    )(x, mod, norm2, wg, wu, wd)


MOE_TM = 4 * MXU_COLS


def _tile_mod(mod_a_ref, mod_b_ref, row, tm, mod_chunk):
    i = pl.program_id(0)
    first_of_b = ((i + 1) * tm - 1) // mod_chunk * mod_chunk
    token = i * tm + lax.broadcasted_iota(jnp.int32, (tm, 1), 0)
    return jnp.where(token >= first_of_b, mod_b_ref[0, row:row + 1, :], mod_a_ref[0, row:row + 1, :])


def _router_kernel(x_ref, mod_a_ref, mod_b_ref, n2_ref, rw_ref, ht_ref, gate_ref, rank_ref, rankt_ref, cnt_ref,
                   *, mod_chunk):
    tm = x_ref.shape[0]
    h = _modulated_norm(x_ref[...], n2_ref[...], _tile_mod(mod_a_ref, mod_b_ref, 3, tm, mod_chunk),
                        _tile_mod(mod_a_ref, mod_b_ref, 4, tm, mod_chunk))
    ht_ref[...] = h.T.astype(BF16)
    logits = jnp.dot(h, rw_ref[...], preferred_element_type=F32, precision=lax.Precision.HIGHEST)
    lane = lax.broadcasted_iota(jnp.int32, logits.shape, 1).astype(F32)
    neg = jnp.float32(-jnp.inf)
    lg = jnp.where(lane < N_EXPERTS, logits, neg)
    m1 = jnp.max(lg, axis=-1, keepdims=True)
    i1 = jnp.min(jnp.where(lg == m1, lane, float(LANES)), axis=-1, keepdims=True)
    lg2 = jnp.where(lane == i1, neg, lg)
    m2 = jnp.max(lg2, axis=-1, keepdims=True)
    i2 = jnp.min(jnp.where(lg2 == m2, lane, float(LANES)), axis=-1, keepdims=True)
    e2 = jnp.exp(m2 - m1)
    den = 1.0 + e2
    gate_ref[...] = jnp.where(lane == i1, 1.0 / den, 0.0) + jnp.where(lane == i2, e2 / den, 0.0)
    sel = (lane == i1) | (lane == i2)
    sel_f = jnp.where(sel, 1.0, 0.0)
    before = (lax.broadcasted_iota(jnp.int32, (tm, tm), 1)
              < lax.broadcasted_iota(jnp.int32, (tm, tm), 0))
    slot = jnp.dot(jnp.where(before, 1.0, 0.0).astype(BF16), sel_f.astype(BF16), preferred_element_type=F32)
    rank = jnp.where(sel, slot, -1.0)
    rank_ref[...] = rank
    rankt_ref[...] = rank.T[:SUBLANES, :]
    cnt = jnp.sum(sel_f, axis=0, keepdims=True)
    cnt_ref[...] = jnp.broadcast_to(cnt, (SUBLANES, LANES)).astype(jnp.int32)


def _router(x, mod, norm2, rw, *, tm=MOE_TM, mod_chunk=MOD_CHUNK):
    n, d = x.shape
    nt = n // tm
    return pl.pallas_call(
        functools.partial(_router_kernel, mod_chunk=mod_chunk),
        grid=(nt,),
        in_specs=[pl.BlockSpec((tm, d), lambda i: (i, 0)),
                  pl.BlockSpec((1, N_MOD, d), lambda i: (i * tm // mod_chunk, 0, 0)),
                  pl.BlockSpec((1, N_MOD, d), lambda i: (((i + 1) * tm - 1) // mod_chunk, 0, 0)),
                  pl.BlockSpec((1, d), lambda i: (0, 0)),
                  pl.BlockSpec((d, LANES), lambda i: (0, 0))],
        out_specs=[pl.BlockSpec((d, tm), lambda i: (0, i)),
                   pl.BlockSpec((tm, LANES), lambda i: (i, 0)),
                   pl.BlockSpec((tm, LANES), lambda i: (i, 0)),
                   pl.BlockSpec((SUBLANES, tm), lambda i: (i, 0)),
                   pl.BlockSpec((SUBLANES, LANES), lambda i: (i, 0))],
        out_shape=[jax.ShapeDtypeStruct((d, n), BF16),
                   jax.ShapeDtypeStruct((n, LANES), F32),
                   jax.ShapeDtypeStruct((n, LANES), F32),
                   jax.ShapeDtypeStruct((nt * SUBLANES, tm), F32),
                   jax.ShapeDtypeStruct((nt * SUBLANES, LANES), jnp.int32)],
        compiler_params=_params(("arbitrary",)),
        name="moe_router",
    )(x, mod, mod, norm2, rw)


def _moe_kernel(cnt_ref, ht_ref, rankt_ref, rank_ref, gate_ref, x_ref, mod_a_ref, mod_b_ref, wg_ref, wu_ref, wd_ref,
                o_ref, xt_scr, yt_scr, *, mod_chunk):
    i, e, f = pl.program_id(0), pl.program_id(1), pl.program_id(2)
    last_f = pl.num_programs(2) - 1
    tm = o_ref.shape[0]
    n_blk = (cnt_ref[i * N_EXPERTS + e] + MXU_COLS - 1) // MXU_COLS

    @pl.when((e == 0) & (f == 0))
    def _():
        o_ref[...] = jnp.zeros_like(o_ref)

    @pl.when(f == 0)
    def _():
        slot = rankt_ref[pl.ds(e, 1), :].astype(jnp.int32)
        sub = lax.broadcasted_iota(jnp.int32, (MXU_COLS, tm), 0)

        def gather(b, carry):
            pick = jnp.where(sub + b * MXU_COLS == slot, 1.0, 0.0).astype(BF16)
            xt_scr[b] = lax.dot_general(ht_ref[...], pick, NT_DIMS, preferred_element_type=F32).astype(BF16)
            yt_scr[b] = jnp.zeros(yt_scr.shape[1:], F32)
            return carry

        lax.fori_loop(0, n_blk, gather, 0)

    def expert(b, carry):
        xb = xt_scr[b]
        g = jnp.dot(wg_ref[0], xb, preferred_element_type=F32)
        u = jnp.dot(wu_ref[0], xb, preferred_element_type=F32)
        act = (g * _sigmoid(g) * u).astype(BF16)
        yt_scr[b] += jnp.dot(wd_ref[0], act, preferred_element_type=F32)
        return carry

    lax.fori_loop(0, n_blk, expert, 0)

    @pl.when(f == last_f)
    def _():
        on_lane = lax.broadcasted_iota(jnp.int32, (tm, LANES), 1) == e
        slot = jnp.sum(jnp.where(on_lane, rank_ref[...], 0.0), axis=-1, keepdims=True).astype(jnp.int32)
        gate = jnp.sum(jnp.where(on_lane, gate_ref[...], 0.0), axis=-1, keepdims=True)
        col = lax.broadcasted_iota(jnp.int32, (tm, MXU_COLS), 1)

        def scatter(b, carry):
            pick = jnp.where(col + b * MXU_COLS == slot, 1.0, 0.0).astype(BF16)
            y = lax.dot_general(pick, yt_scr[b].astype(BF16), NT_DIMS, preferred_element_type=F32)
            o_ref[...] += gate * y
            return carry

        lax.fori_loop(0, n_blk, scatter, 0)

    @pl.when((e == pl.num_programs(1) - 1) & (f == last_f))
    def _():
        o_ref[...] = x_ref[...] + _tile_mod(mod_a_ref, mod_b_ref, 5, tm, mod_chunk) * o_ref[...]


def _moe(counts, ht, rankt, rank, gate, x, mod, wgt, wut, wdt, *, tm=MOE_TM, tf=896, mod_chunk=MOD_CHUNK):
    n, d = x.shape
    n_exp, d_ff, _ = wgt.shape
    grid_spec = pltpu.PrefetchScalarGridSpec(
        num_scalar_prefetch=1,
        grid=(n // tm, n_exp, d_ff // tf),
        in_specs=[pl.BlockSpec((d, tm), lambda i, e, f, c: (0, i)),
                  pl.BlockSpec((SUBLANES, tm), lambda i, e, f, c: (i, 0)),
                  pl.BlockSpec((tm, LANES), lambda i, e, f, c: (i, 0)),
                  pl.BlockSpec((tm, LANES), lambda i, e, f, c: (i, 0)),
                  pl.BlockSpec((tm, d), lambda i, e, f, c: (i, 0)),
                  pl.BlockSpec((1, N_MOD, d), lambda i, e, f, c: (i * tm // mod_chunk, 0, 0)),
                  pl.BlockSpec((1, N_MOD, d), lambda i, e, f, c: (((i + 1) * tm - 1) // mod_chunk, 0, 0)),
                  pl.BlockSpec((1, tf, d), lambda i, e, f, c: (e, f, 0)),
                  pl.BlockSpec((1, tf, d), lambda i, e, f, c: (e, f, 0)),
                  pl.BlockSpec((1, d, tf), lambda i, e, f, c: (e, 0, f))],
        out_specs=pl.BlockSpec((tm, d), lambda i, e, f, c: (i, 0)),
        scratch_shapes=[pltpu.VMEM((tm // MXU_COLS, d, MXU_COLS), BF16),
                        pltpu.VMEM((tm // MXU_COLS, d, MXU_COLS), F32)])
    return pl.pallas_call(
        functools.partial(_moe_kernel, mod_chunk=mod_chunk),
        grid_spec=grid_spec,
        out_shape=jax.ShapeDtypeStruct((n, d), F32),
        compiler_params=_params(("arbitrary", "arbitrary", "arbitrary")),
        name="moe_experts",
    )(counts, ht, rankt, rank, gate, x, mod, mod, wgt, wut, wdt)


def _moe_layer(x, mod, norm2, router_w, w_gate, w_up, w_down):
    rw = jnp.pad(router_w, ((0, 0), (0, LANES - N_EXPERTS)))
    ht, gate, rank, rankt, cnt = _router(x, mod, norm2, rw)
    counts = cnt.reshape(-1, SUBLANES, LANES)[:, 0, :N_EXPERTS].reshape(-1)
    wgt = jnp.swapaxes(w_gate.astype(BF16), 1, 2)
    wut = jnp.swapaxes(w_up.astype(BF16), 1, 2)
    wdt = jnp.swapaxes(w_down.astype(BF16), 1, 2)
    return _moe(counts, ht, rankt, rank, gate, x, mod, wgt, wut, wdt)


def _final_kernel(x_ref, g_ref, prompt_ref, sample_ref, *, prompt_tiles):
    x = x_ref[...]
    ms = jnp.mean(x * x, axis=-1, keepdims=True)
    y = x * lax.rsqrt(ms + NORM_EPS) * g_ref[...]
    i = pl.program_id(0)

    @pl.when(i < prompt_tiles)
    def _():
        prompt_ref[...] = y

    @pl.when(i >= prompt_tiles)
    def _():
        sample_ref[...] = y


def _final_norm(x, gain):
    tm = 1024
    prompt_tiles = N_PROMPT // tm
    return pl.pallas_call(
        functools.partial(_final_kernel, prompt_tiles=prompt_tiles),
        grid=(N_TOK // tm,),
        in_specs=[pl.BlockSpec((tm, D_MODEL), lambda i: (i, 0)),
                  pl.BlockSpec((1, D_MODEL), lambda i: (0, 0))],
        out_specs=[pl.BlockSpec((tm, D_MODEL), lambda i: (jnp.minimum(i, prompt_tiles - 1), 0)),
                   pl.BlockSpec((tm, D_MODEL), lambda i: (jnp.maximum(i - prompt_tiles, 0), 0))],
        out_shape=[jax.ShapeDtypeStruct((N_PROMPT, D_MODEL), F32),
                   jax.ShapeDtypeStruct((N_SAMPLE, D_MODEL), F32)],
        compiler_params=_params(("arbitrary",)),
        name="final_norm",
    )(x, gain)


def _rope_tables():
    pos = jnp.arange(DEC_SEQ, dtype=jnp.int32)
    d = jnp.arange(LANES, dtype=jnp.int32) % DIFF_HEAD_DIM

    def angles(p, theta, idx, half):
        inv_freq = jnp.exp(-math.log(theta) * jnp.arange(half, dtype=F32) / half)
        return p.astype(F32)[:, None] * inv_freq[idx][None, :]

    half_d = DIFF_ROT_DIM // 2
    ang = angles(pos, ROPE_THETA, d % half_d, half_d)
    active = (d < DIFF_ROT_DIM)[None, :]
    first = (d < half_d)[None, :]
    tab_d = jnp.stack([jnp.where(active, jnp.cos(ang), 1.0),
                       jnp.where(first, -jnp.sin(ang), 0.0),
                       jnp.where(active & ~first, jnp.sin(ang), 0.0)])

    half_g = GQA_HEAD_DIM // 4
    use_row = (d < GQA_HEAD_DIM // 2)[None, :]
    ang_r = angles(pos // GRID_W, AXIAL_THETA, d % half_g, half_g)
    ang_c = angles(pos % GRID_W, AXIAL_THETA, d % half_g, half_g)
    ang = jnp.where(use_row, ang_r, ang_c)
    first = ((d % (2 * half_g)) < half_g)[None, :]
    tab_g = jnp.stack([jnp.cos(ang),
                       jnp.where(first, -jnp.sin(ang), 0.0),
                       jnp.where(first, 0.0, jnp.sin(ang))])
    return tab_d.astype(F32), tab_g.astype(F32)


def _extend_w_in(w):
    def dup(cols):
        c = cols.reshape(D_MODEL, GQA_KV_HEADS, 1, GQA_HEAD_DIM)
        return jnp.broadcast_to(c, (D_MODEL, GQA_KV_HEADS, 2, GQA_HEAD_DIM)).reshape(D_MODEL, -1)
    kv = GQA_KV_HEADS * GQA_HEAD_DIM
    gk = w[:, COL_GK:COL_GK + kv]
    gv = w[:, COL_GK + kv:COL_GK + 2 * kv]
    return jnp.concatenate([w[:, :COL_GK], dup(gk), dup(gv), w[:, COL_GK + 2 * kv:]], axis=1)


def kernel(x_prompt, x_sample, c_prompt, c_sample, w_ada, b_ada, norm1, w_in, diff_lambda, diff_subln,
           gqa_q_norm, gqa_k_norm, w_out, norm2, ffn_w_gate, ffn_w_up, ffn_w_down, router_w,
           moe_w_gate, moe_w_up, moe_w_down, final_norm):
    x = jnp.concatenate([x_prompt.reshape(N_PROMPT, D_MODEL), x_sample.reshape(N_SAMPLE, D_MODEL)], axis=0)
    n_c = BATCH + DEC_BATCH
    c_all = jnp.concatenate([c_prompt, c_sample, jnp.zeros((16 - n_c, D_MODEL), F32)], axis=0)
    mod_all = _ada(c_all, w_ada, b_ada)
    chunk_src = jnp.asarray(list(range(BATCH)) + [BATCH + i // 2 for i in range(2 * DEC_BATCH)], jnp.int32)
    tab_d, tab_g = _rope_tables()
    grp = jnp.kron(jnp.eye(MXU_COLS // GQA_HEAD_DIM, dtype=F32),
                   jnp.full((GQA_HEAD_DIM, GQA_HEAD_DIM), 1.0 / GQA_HEAD_DIM, F32)).astype(BF16)

    for l in range(DEPTH):
        mod = mod_all[l][chunk_src].reshape(N_CHUNKS, N_MOD, D_MODEL)
        lam_init = 0.8 - 0.6 * math.exp(-0.3 * l)
        w_ext = _extend_w_in(w_in[l]).astype(BF16)
        gains = jnp.stack([jnp.tile(gqa_q_norm[l], PROJ_TN // GQA_HEAD_DIM),
                           jnp.tile(gqa_k_norm[l], PROJ_TN // GQA_HEAD_DIM)])
        proj = _proj(x, mod, norm1[l][None, :], w_ext, tab_d, tab_g, gains, grp)
        subln = diff_subln[l][None, :]
        a, b = None, None
        for batch, seq, row0 in ((BATCH, SEQ, 0), (DEC_BATCH, DEC_SEQ, N_PROMPT)):
            a = _attention(proj, diff=True, batch=batch, seq=seq, row0=row0, lam_init=lam_init,
                           lam=diff_lambda[l], subln=subln, out=a)
            b = _attention(proj, diff=False, batch=batch, seq=seq, row0=row0, out=b)
        x = _merge(proj, a, b, x, mod, w_out[l].astype(BF16))
        if l % 2 == 0:
            x = _ffn(x, mod, norm2[l][None, :], ffn_w_gate[l // 2].astype(BF16),
                     ffn_w_up[l // 2].astype(BF16), ffn_w_down[l // 2].astype(BF16))
        else:
            x = _moe_layer(x, mod, norm2[l][None, :], router_w[l // 2], moe_w_gate[l // 2],
                           moe_w_up[l // 2], moe_w_down[l // 2])
    y_prompt, y_sample = _final_norm(x, final_norm[None, :])
    return (y_prompt.reshape(BATCH, SEQ, D_MODEL), y_sample.reshape(DEC_BATCH, DEC_SEQ, D_MODEL))
```

```python
import functools
import math

import jax
import jax.numpy as jnp
from jax import lax
from jax.experimental import pallas as pl
from jax.experimental.pallas import tpu as pltpu

F32 = jnp.float32
BF16 = jnp.bfloat16

D_MODEL = 1024
BATCH = 8
SEQ = 4096
DEPTH = 4
DEC_BATCH = 2
DEC_SEQ = 8192
GRID_W = 64
DIFF_HEADS = 8
DIFF_HEAD_DIM = 64
DIFF_ROT_DIM = DIFF_HEAD_DIM // 4
ROPE_THETA = 500000.0
GQA_Q_HEADS = 16
GQA_KV_HEADS = 4
GQA_HEAD_DIM = 64
AXIAL_THETA = 10000.0
D_FF_DENSE = 2816
N_EXPERTS = 8
D_FF_EXPERT = 3584
NORM_EPS = 1e-6
N_MOD = 6

N_PROMPT = BATCH * SEQ
N_SAMPLE = DEC_BATCH * DEC_SEQ
N_TOK = N_PROMPT + N_SAMPLE
MOD_CHUNK = SEQ
N_CHUNKS = N_TOK // MOD_CHUNK

LANES = 128
SUBLANES = 8
MXU_COLS = 256
VMEM_LIMIT = 56 * 1024 * 1024

Q_SCALE = DIFF_HEAD_DIM ** -0.5 * math.log2(math.e)

PROJ_TN = 512
COL_DQ, COL_DK, COL_DV, COL_GQ = 0, 1024, 2048, 3072
COL_GK, COL_GV, COL_GA, COL_GB = 4096, 4608, 5120, 6144
PROJ_W = 7168
N_PROJ_TILES = PROJ_W // PROJ_TN

NT_DIMS = (((1,), (1,)), ((), ()))


def _params(sem, vmem=VMEM_LIMIT):
    return pltpu.CompilerParams(dimension_semantics=sem, vmem_limit_bytes=vmem)


def _sigmoid(x):
    return 0.5 * jnp.tanh(0.5 * x) + 0.5


def _ada_kernel(c_ref, w_ref, b_ref, o_ref):
    c = c_ref[...]
    ca = c * _sigmoid(c)
    o_ref[0] = jnp.dot(ca.astype(BF16), w_ref[0].astype(BF16), preferred_element_type=F32) + b_ref[0]


def _ada(c_all, w_ada, b_ada):
    rows = c_all.shape[0]
    tn = 1536
    width = N_MOD * D_MODEL
    return pl.pallas_call(
        _ada_kernel,
        grid=(DEPTH, width // tn),
        in_specs=[pl.BlockSpec((rows, D_MODEL), lambda l, j: (0, 0)),
                  pl.BlockSpec((1, D_MODEL, tn), lambda l, j: (l, 0, j)),
                  pl.BlockSpec((1, 1, tn), lambda l, j: (l, 0, j))],
        out_specs=pl.BlockSpec((1, rows, tn), lambda l, j: (l, 0, j)),
        out_shape=jax.ShapeDtypeStruct((DEPTH, rows, width), F32),
        compiler_params=_params(("arbitrary", "arbitrary")),
        name="ada_mod",
    )(c_all, w_ada, b_ada.reshape(DEPTH, 1, width))


def _modulated_norm(x, gain, shift, scale):
    ms = jnp.mean(x * x, axis=-1, keepdims=True)
    y = x * lax.rsqrt(ms + NORM_EPS) * gain
    return y * (1.0 + scale) + shift


def _rotate(x, c, sm, sp, shift):
    return x * c + pltpu.roll(x, LANES - shift, axis=1) * sm + pltpu.roll(x, shift, axis=1) * sp


def _pos_block(i, tm):
    return jnp.where(i < N_PROMPT // tm, i % (SEQ // tm), i % (DEC_SEQ // tm))


def _proj_kernel(x_ref, mod_ref, n1_ref, w_ref, tabd_ref, tabg_ref, gain_ref, grp_ref, o_ref, h_scr, acc_a, acc_b):
    j = pl.program_id(1)
    tile = j - 1
    n_slab = PROJ_TN // LANES
    q_tiles = DIFF_HEADS * 2 * DIFF_HEAD_DIM // PROJ_TN

    def matmul(dst):
        dst[...] = jnp.dot(h_scr[...], w_ref[...], preferred_element_type=F32)

    def rope_epilogue(src):
        c, sm, sp = tabd_ref[0], tabd_ref[1], tabd_ref[2]
        scale = jnp.where(tile < q_tiles, Q_SCALE, 1.0)
        for s in range(n_slab):
            xs = src[:, s * LANES:(s + 1) * LANES] * scale
            o_ref[:, s * LANES:(s + 1) * LANES] = _rotate(xs, c, sm, sp, DIFF_ROT_DIM // 2).astype(BF16)

    def plain_epilogue(src):
        o_ref[...] = src[...].astype(BF16)

    def norm_rope_epilogue(src):
        is_q = tile < COL_GK // PROJ_TN
        gain = gain_ref[pl.ds(jnp.where(is_q, 0, 1), 1), :] * jnp.where(is_q, Q_SCALE, 1.0)
        c, sm, sp = tabg_ref[0], tabg_ref[1], tabg_ref[2]
        for t in range(PROJ_TN // MXU_COLS):
            a = src[:, t * MXU_COLS:(t + 1) * MXU_COLS]
            sq = a * a
            hi = sq.astype(BF16)
            lo = (sq - hi.astype(F32)).astype(BF16)
            ms = (jnp.dot(hi, grp_ref[...], preferred_element_type=F32)
                  + jnp.dot(lo, grp_ref[...], preferred_element_type=F32))
            y = a * lax.rsqrt(ms + NORM_EPS) * gain[:, t * MXU_COLS:(t + 1) * MXU_COLS]
            for s in range(MXU_COLS // LANES):
                col = t * MXU_COLS + s * LANES
                ys = y[:, s * LANES:(s + 1) * LANES]
                o_ref[:, col:col + LANES] = _rotate(ys, c, sm, sp, GQA_HEAD_DIM // 4).astype(BF16)

    def gate_epilogue(src):
        for s in range(n_slab):
            o_ref[:, s * LANES:(s + 1) * LANES] = _sigmoid(src[:, s * LANES:(s + 1) * LANES]).astype(BF16)

    is_plain = ((tile >= COL_DV // PROJ_TN) & (tile < COL_GQ // PROJ_TN)) | (tile == COL_GV // PROJ_TN)
    kinds = ((tile < COL_DV // PROJ_TN, rope_epilogue),
             (is_plain, plain_epilogue),
             ((tile >= COL_GQ // PROJ_TN) & (tile < COL_GV // PROJ_TN), norm_rope_epilogue),
             (tile >= COL_GA // PROJ_TN, gate_epilogue))

    @pl.when(j == 0)
    def _():
        h = _modulated_norm(x_ref[...], n1_ref[...], mod_ref[0, 0:1, :], mod_ref[0, 1:2, :])
        h_scr[...] = h.astype(BF16)
        matmul(acc_a)

    for parity, cur, prev in ((0, acc_a, acc_b), (1, acc_b, acc_a)):
        for cond, epilogue in kinds:
            @pl.when((j > 0) & (j < N_PROJ_TILES) & (j % 2 == parity) & cond)
            def _(cur=cur, prev=prev, epilogue=epilogue):
                epilogue(prev)
                matmul(cur)

    @pl.when(j == N_PROJ_TILES)
    def _():
        gate_epilogue(acc_a if N_PROJ_TILES % 2 else acc_b)


def _proj(x, mod, norm1, w_ext, tab_d, tab_g, gains, grp):
    tm = 1024
    last = N_PROJ_TILES - 1
    return pl.pallas_call(
        _proj_kernel,
        grid=(N_TOK // tm, N_PROJ_TILES + 1),
        in_specs=[pl.BlockSpec((tm, D_MODEL), lambda i, j: (i, 0)),
                  pl.BlockSpec((1, N_MOD, D_MODEL), lambda i, j: (i * tm // MOD_CHUNK, 0, 0)),
                  pl.BlockSpec((1, D_MODEL), lambda i, j: (0, 0)),
                  pl.BlockSpec((D_MODEL, PROJ_TN), lambda i, j: (0, jnp.minimum(j, last))),
                  pl.BlockSpec((3, tm, LANES), lambda i, j: (0, _pos_block(i, tm), 0)),
                  pl.BlockSpec((3, tm, LANES), lambda i, j: (0, _pos_block(i, tm), 0)),
                  pl.BlockSpec((2, PROJ_TN), lambda i, j: (0, 0)),
                  pl.BlockSpec((MXU_COLS, MXU_COLS), lambda i, j: (0, 0))],
        out_specs=pl.BlockSpec((tm, PROJ_TN), lambda i, j: (i, jnp.maximum(j - 1, 0))),
        out_shape=jax.ShapeDtypeStruct((N_TOK, PROJ_W), BF16),
        scratch_shapes=[pltpu.VMEM((tm, D_MODEL), BF16), pltpu.VMEM((tm, PROJ_TN), F32),
                        pltpu.VMEM((tm, PROJ_TN), F32)],
        compiler_params=_params(("arbitrary", "arbitrary")),
        name="proj_in",
    )(x, mod, norm1, w_ext, tab_d, tab_g, gains, grp)


ATT_SCORE_BYTES = 8 * 1024 * 1024


def _attn_kernel(*refs, diff, lam_init, nq, aliased):
    q_ref, k_ref, v_ref = refs[:3]
    if diff:
        lam_ref, subln_ref = refs[3:5]
    n_in = 3 + (2 if diff else 0) + (1 if aliased else 0)
    o_ref, bufs, vx_ref = refs[n_in], refs[n_in + 1:-1], refs[-1]
    set_a, set_b = bufs[:4], bufs[4:]
    half = LANES // 2
    step = pl.program_id(0)

    @pl.when(step == 0)
    def _():
        for ref in set_b:
            ref[...] = jnp.zeros_like(ref)

    @pl.when(jnp.maximum(step - 1, 0) % nq == 0)
    def _():
        v = v_ref[...]
        if diff:
            vx_ref[:, :LANES] = v
            vx_ref[:, LANES:] = jnp.ones_like(v)
        else:
            lane = lax.broadcasted_iota(jnp.int32, v.shape, 1)
            vx_ref[...] = jnp.where(lane < half, v, jnp.ones_like(v))

    def finish(s_ref, m_ref):
        m = jnp.max(m_ref[...], axis=-1, keepdims=True)
        p = jnp.exp2((s_ref[...] - m).astype(BF16))
        acc = jnp.dot(p, vx_ref[...], preferred_element_type=F32)
        return acc[:, :LANES] / acc[:, LANES:] if diff else acc

    def start(qh, s_ref, m_ref):
        s = lax.dot_general(qh, k_ref[...], NT_DIMS, preferred_element_type=F32)
        s_ref[...] = s
        top = s[:, :LANES]
        for u in range(1, s.shape[1] // LANES):
            top = jnp.maximum(top, s[:, u * LANES:(u + 1) * LANES])
        m_ref[...] = top

    def run(cur, prev):
        q = q_ref[...]
        low = lax.broadcasted_iota(jnp.int32, q.shape, 1) < half
        zero = jnp.zeros_like(q)
        acc_lo = finish(prev[0], prev[2])
        start(jnp.where(low, q, zero), cur[0], cur[2])
        acc_hi = finish(prev[1], prev[3])
        if diff:
            lp = lam_ref[...]
            t1 = jnp.sum(lp[0:1] * lp[1:2], axis=-1, keepdims=True)
            t2 = jnp.sum(lp[2:3] * lp[3:4], axis=-1, keepdims=True)
            lam = jnp.exp(t1) - jnp.exp(t2) + lam_init
            a = acc_lo - lam * acc_hi
            ms = jnp.mean(a * a, axis=-1, keepdims=True)
            a = a * lax.rsqrt(ms + NORM_EPS) * subln_ref[...] * (1.0 - lam_init)
            o_ref[...] = a.astype(BF16)
        else:
            out = jnp.where(low, acc_lo / pltpu.roll(acc_lo, half, axis=1),
                            pltpu.roll(acc_hi, half, axis=1) / acc_hi)
            o_ref[...] = out.astype(BF16)
        start(jnp.where(low, zero, q), cur[1], cur[3])

    @pl.when(step % 2 == 0)
    def _():
        run(set_a, set_b)

    @pl.when(step % 2 == 1)
    def _():
        run(set_b, set_a)


def _attention(proj, *, diff, batch, seq, row0, lam_init=0.0, lam=None, subln=None, out=None):
    tq = ATT_SCORE_BYTES // (4 * seq)
    nq = seq // tq
    q0 = row0 // tq
    s0 = row0 // seq
    n_units = DIFF_HEADS
    n_blocks = batch * n_units * nq
    kv_unit = (lambda h: h) if diff else (lambda h: h // 2)
    qc, kc, vc = ((COL_DQ, COL_DK, COL_DV) if diff else (COL_GQ, COL_GK, COL_GV))

    def block(g):
        return g // (n_units * nq), (g // nq) % n_units, g % nq

    def started(g):
        return block(jnp.minimum(g, n_blocks - 1))

    def finished(g):
        return block(jnp.maximum(g - 1, 0))

    def q_map(g):
        b, h, i = started(g)
        return q0 + b * nq + i, qc // LANES + h

    def k_map(g):
        b, h, _ = started(g)
        return s0 + b, kc // LANES + kv_unit(h)

    def v_map(g):
        b, h, _ = finished(g)
        return s0 + b, vc // LANES + kv_unit(h)

    def o_map(g):
        b, h, i = finished(g)
        return q0 + b * nq + i, h

    in_specs = [pl.BlockSpec((tq, LANES), q_map),
                pl.BlockSpec((seq, LANES), k_map),
                pl.BlockSpec((seq, LANES), v_map)]
    args = [proj, proj, proj]
    if diff:
        in_specs += [pl.BlockSpec((4, DIFF_HEAD_DIM), lambda g: (0, 0)),
                     pl.BlockSpec((1, LANES), lambda g: (0, 0))]
        args += [lam, subln]
    aliases = {}
    if out is not None:
        aliases = {len(args): 0}
        in_specs.append(pl.BlockSpec(memory_space=pl.ANY))
        args.append(out)
    return pl.pallas_call(
        functools.partial(_attn_kernel, diff=diff, lam_init=lam_init, nq=nq, aliased=out is not None),
        grid=(n_blocks + 1,),
        in_specs=in_specs,
        out_specs=pl.BlockSpec((tq, LANES), o_map),
        out_shape=jax.ShapeDtypeStruct((N_TOK, D_MODEL), BF16),
        input_output_aliases=aliases,
        scratch_shapes=(([pltpu.VMEM((tq, seq), F32)] * 2 + [pltpu.VMEM((tq, LANES), F32)] * 2) * 2
                        + [pltpu.VMEM((seq, 2 * LANES if diff else LANES), BF16)]),
        compiler_params=_params(("arbitrary",)),
        name=("diff_attn" if diff else "gqa_attn") + f"_{seq}",
    )(*args)


def _merge_kernel(ga_ref, gb_ref, a_ref, b_ref, x_ref, mod_ref, w_ref, o_ref):
    merged = (ga_ref[...].astype(F32) * a_ref[...].astype(F32)
              + gb_ref[...].astype(F32) * b_ref[...].astype(F32))
    y = jnp.dot(merged.astype(BF16), w_ref[...], preferred_element_type=F32)
    o_ref[...] = x_ref[...] + mod_ref[0, 2:3, :] * y


def _merge(proj, a, b, x, mod, w_out):
    tm = 512
    row = lambda i: (i, 0)
    return pl.pallas_call(
        _merge_kernel,
        grid=(N_TOK // tm,),
        in_specs=[pl.BlockSpec((tm, D_MODEL), lambda i: (i, COL_GA // D_MODEL)),
                  pl.BlockSpec((tm, D_MODEL), lambda i: (i, COL_GB // D_MODEL)),
                  pl.BlockSpec((tm, D_MODEL), row),
                  pl.BlockSpec((tm, D_MODEL), row),
                  pl.BlockSpec((tm, D_MODEL), row),
                  pl.BlockSpec((1, N_MOD, D_MODEL), lambda i: (i * tm // MOD_CHUNK, 0, 0)),
                  pl.BlockSpec((D_MODEL, D_MODEL), lambda i: (0, 0))],
        out_specs=pl.BlockSpec((tm, D_MODEL), row),
        out_shape=jax.ShapeDtypeStruct((N_TOK, D_MODEL), F32),
        compiler_params=_params(("arbitrary",)),
        name="merge_out",
    )(proj, proj, a, b, x, mod, w_out)


def _ffn_kernel(x_ref, mod_ref, n2_ref, wg_ref, wu_ref, wd_ref, o_ref, h_scr):
    f = pl.program_id(1)

    @pl.when(f == 0)
    def _():
        h = _modulated_norm(x_ref[...], n2_ref[...], mod_ref[0, 3:4, :], mod_ref[0, 4:5, :])
        h_scr[...] = h.astype(BF16)
        o_ref[...] = jnp.zeros_like(o_ref)

    h = h_scr[...]
    g = jnp.dot(h, wg_ref[...], preferred_element_type=F32)
    u = jnp.dot(h, wu_ref[...], preferred_element_type=F32)
    act = (g * _sigmoid(g) * u).astype(BF16)
    o_ref[...] += jnp.dot(act, wd_ref[...], preferred_element_type=F32)

    @pl.when(f == pl.num_programs(1) - 1)
    def _():
        o_ref[...] = x_ref[...] + mod_ref[0, 5:6, :] * o_ref[...]


def _ffn(x, mod, norm2, wg, wu, wd):
    tm, tf = 1024, 256
    return pl.pallas_call(
        _ffn_kernel,
        grid=(N_TOK // tm, D_FF_DENSE // tf),
        in_specs=[pl.BlockSpec((tm, D_MODEL), lambda i, f: (i, 0)),
                  pl.BlockSpec((1, N_MOD, D_MODEL), lambda i, f: (i * tm // MOD_CHUNK, 0, 0)),
                  pl.BlockSpec((1, D_MODEL), lambda i, f: (0, 0)),
                  pl.BlockSpec((D_MODEL, tf), lambda i, f: (0, f)),
                  pl.BlockSpec((D_MODEL, tf), lambda i, f: (0, f)),
                  pl.BlockSpec((tf, D_MODEL), lambda i, f: (f, 0))],
        out_specs=pl.BlockSpec((tm, D_MODEL), lambda i, f: (i, 0)),
        out_shape=jax.ShapeDtypeStruct((N_TOK, D_MODEL), F32),
        scratch_shapes=[pltpu.VMEM((tm, D_MODEL), BF16)],
        compiler_params=_params(("arbitrary", "arbitrary")),
        name="ffn_dense",
    )(x, mod, norm2, wg, wu, wd)


MOE_TM = 4 * MXU_COLS
MOE_FF_SUB = 896


def _tile_mod(mod_a_ref, mod_b_ref, row, tm, mod_chunk):
    i = pl.program_id(0)
    first_of_b = ((i + 1) * tm - 1) // mod_chunk * mod_chunk
    token = i * tm + lax.broadcasted_iota(jnp.int32, (tm, 1), 0)
    return jnp.where(token >= first_of_b, mod_b_ref[0, row:row + 1, :], mod_a_ref[0, row:row + 1, :])


def _router_kernel(x_ref, mod_a_ref, mod_b_ref, n2_ref, rw_ref, ht_ref, gate_ref, rank_ref, rankt_ref, cnt_ref,
                   *, mod_chunk):
    tm = x_ref.shape[0]
    h = _modulated_norm(x_ref[...], n2_ref[...], _tile_mod(mod_a_ref, mod_b_ref, 3, tm, mod_chunk),
                        _tile_mod(mod_a_ref, mod_b_ref, 4, tm, mod_chunk))
    ht_ref[...] = h.T.astype(BF16)
    logits = jnp.dot(h, rw_ref[...], preferred_element_type=F32, precision=lax.Precision.HIGHEST)
    lane = lax.broadcasted_iota(jnp.int32, logits.shape, 1).astype(F32)
    neg = jnp.float32(-jnp.inf)
    lg = jnp.where(lane < N_EXPERTS, logits, neg)
    m1 = jnp.max(lg, axis=-1, keepdims=True)
    i1 = jnp.min(jnp.where(lg == m1, lane, float(LANES)), axis=-1, keepdims=True)
    lg2 = jnp.where(lane == i1, neg, lg)
    m2 = jnp.max(lg2, axis=-1, keepdims=True)
    i2 = jnp.min(jnp.where(lg2 == m2, lane, float(LANES)), axis=-1, keepdims=True)
    e2 = jnp.exp(m2 - m1)
    den = 1.0 + e2
    gate_ref[...] = jnp.where(lane == i1, 1.0 / den, 0.0) + jnp.where(lane == i2, e2 / den, 0.0)
    sel = (lane == i1) | (lane == i2)
    sel_f = jnp.where(sel, 1.0, 0.0)
    before = (lax.broadcasted_iota(jnp.int32, (tm, tm), 1)
              < lax.broadcasted_iota(jnp.int32, (tm, tm), 0))
    slot = jnp.dot(jnp.where(before, 1.0, 0.0).astype(BF16), sel_f.astype(BF16), preferred_element_type=F32)
    rank = jnp.where(sel, slot, -1.0)
    rank_ref[...] = rank
    rankt_ref[...] = rank.T[:SUBLANES, :]
    cnt = jnp.sum(sel_f, axis=0, keepdims=True)
    cnt_ref[...] = jnp.broadcast_to(cnt, (SUBLANES, LANES)).astype(jnp.int32)


def _router(x, mod, norm2, rw, *, tm=MOE_TM, mod_chunk=MOD_CHUNK):
    n, d = x.shape
    nt = n // tm
    return pl.pallas_call(
        functools.partial(_router_kernel, mod_chunk=mod_chunk),
        grid=(nt,),
        in_specs=[pl.BlockSpec((tm, d), lambda i: (i, 0)),
                  pl.BlockSpec((1, N_MOD, d), lambda i: (i * tm // mod_chunk, 0, 0)),
                  pl.BlockSpec((1, N_MOD, d), lambda i: (((i + 1) * tm - 1) // mod_chunk, 0, 0)),
                  pl.BlockSpec((1, d), lambda i: (0, 0)),
                  pl.BlockSpec((d, LANES), lambda i: (0, 0))],
        out_specs=[pl.BlockSpec((d, tm), lambda i: (0, i)),
                   pl.BlockSpec((tm, LANES), lambda i: (i, 0)),
                   pl.BlockSpec((tm, LANES), lambda i: (i, 0)),
                   pl.BlockSpec((SUBLANES, tm), lambda i: (i, 0)),
                   pl.BlockSpec((SUBLANES, LANES), lambda i: (i, 0))],
        out_shape=[jax.ShapeDtypeStruct((d, n), BF16),
                   jax.ShapeDtypeStruct((n, LANES), F32),
                   jax.ShapeDtypeStruct((n, LANES), F32),
                   jax.ShapeDtypeStruct((nt * SUBLANES, tm), F32),
                   jax.ShapeDtypeStruct((nt * SUBLANES, LANES), jnp.int32)],
        compiler_params=_params(("arbitrary",)),
        name="moe_router",
    )(x, mod, mod, norm2, rw)


def _moe_kernel(cnt_ref, ht_ref, rankt_ref, rank_ref, gate_ref, x_ref, mod_a_ref, mod_b_ref, wg_ref, wu_ref, wd_ref,
                o_ref, xt_scr, yt_scr, *, mod_chunk):
    i, e, f = pl.program_id(0), pl.program_id(1), pl.program_id(2)
    last_f = pl.num_programs(2) - 1
    tm = o_ref.shape[0]
    n_blk = (cnt_ref[i * N_EXPERTS + e] + MXU_COLS - 1) // MXU_COLS

    @pl.when((e == 0) & (f == 0))
    def _():
        o_ref[...] = jnp.zeros_like(o_ref)

    @pl.when(f == 0)
    def _():
        slot = rankt_ref[pl.ds(e, 1), :].astype(jnp.int32)
        sub = lax.broadcasted_iota(jnp.int32, (MXU_COLS, tm), 0)

        def gather(b, carry):
            pick = jnp.where(sub + b * MXU_COLS == slot, 1.0, 0.0).astype(BF16)
            xt_scr[b] = lax.dot_general(ht_ref[...], pick, NT_DIMS, preferred_element_type=F32).astype(BF16)
            yt_scr[b] = jnp.zeros(yt_scr.shape[1:], F32)
            return carry

        lax.fori_loop(0, n_blk, gather, 0)

    def expert(b, carry):
        xb = xt_scr[b]
        g = jnp.dot(wg_ref[0], xb, preferred_element_type=F32)
        u = jnp.dot(wu_ref[0], xb, preferred_element_type=F32)
        act = (g * _sigmoid(g) * u).astype(BF16)
        yt_scr[b] += jnp.dot(wd_ref[0], act, preferred_element_type=F32)
        return carry

    lax.fori_loop(0, n_blk, expert, 0)

    @pl.when(f == last_f)
    def _():
        on_lane = lax.broadcasted_iota(jnp.int32, (tm, LANES), 1) == e
        slot = jnp.sum(jnp.where(on_lane, rank_ref[...], 0.0), axis=-1, keepdims=True).astype(jnp.int32)
        gate = jnp.sum(jnp.where(on_lane, gate_ref[...], 0.0), axis=-1, keepdims=True)
        col = lax.broadcasted_iota(jnp.int32, (tm, MXU_COLS), 1)

        def scatter(b, carry):
            pick = jnp.where(col + b * MXU_COLS == slot, 1.0, 0.0).astype(BF16)
            y = lax.dot_general(pick, yt_scr[b].astype(BF16), NT_DIMS, preferred_element_type=F32)
            o_ref[...] += gate * y
            return carry

        lax.fori_loop(0, n_blk, scatter, 0)

    @pl.when((e == pl.num_programs(1) - 1) & (f == last_f))
    def _():
        o_ref[...] = x_ref[...] + _tile_mod(mod_a_ref, mod_b_ref, 5, tm, mod_chunk) * o_ref[...]


def _moe(counts, ht, rankt, rank, gate, x, mod, wgt, wut, wdt, *, tm=MOE_TM, tf=MOE_FF_SUB, mod_chunk=MOD_CHUNK):
    n, d = x.shape
    n_exp, d_ff, _ = wgt.shape
    grid_spec = pltpu.PrefetchScalarGridSpec(
        num_scalar_prefetch=1,
        grid=(n // tm, n_exp, d_ff // tf),
        in_specs=[pl.BlockSpec((d, tm), lambda i, e, f, c: (0, i)),
                  pl.BlockSpec((SUBLANES, tm), lambda i, e, f, c: (i, 0)),
                  pl.BlockSpec((tm, LANES), lambda i, e, f, c: (i, 0)),
                  pl.BlockSpec((tm, LANES), lambda i, e, f, c: (i, 0)),
                  pl.BlockSpec((tm, d), lambda i, e, f, c: (i, 0)),
                  pl.BlockSpec((1, N_MOD, d), lambda i, e, f, c: (i * tm // mod_chunk, 0, 0)),
                  pl.BlockSpec((1, N_MOD, d), lambda i, e, f, c: (((i + 1) * tm - 1) // mod_chunk, 0, 0)),
                  pl.BlockSpec((1, tf, d), lambda i, e, f, c: (e, f, 0)),
                  pl.BlockSpec((1, tf, d), lambda i, e, f, c: (e, f, 0)),
                  pl.BlockSpec((1, d, tf), lambda i, e, f, c: (e, 0, f))],
        out_specs=pl.BlockSpec((tm, d), lambda i, e, f, c: (i, 0)),
        scratch_shapes=[pltpu.VMEM((tm // MXU_COLS, d, MXU_COLS), BF16),
                        pltpu.VMEM((tm // MXU_COLS, d, MXU_COLS), F32)])
    return pl.pallas_call(
        functools.partial(_moe_kernel, mod_chunk=mod_chunk),
        grid_spec=grid_spec,
        out_shape=jax.ShapeDtypeStruct((n, d), F32),
        compiler_params=_params(("arbitrary", "arbitrary", "arbitrary")),
        name="moe_experts",
    )(counts, ht, rankt, rank, gate, x, mod, mod, wgt, wut, wdt)


def _moe_layer(x, mod, norm2, router_w, w_gate, w_up, w_down):
    rw = jnp.pad(router_w, ((0, 0), (0, LANES - N_EXPERTS)))
    ht, gate, rank, rankt, cnt = _router(x, mod, norm2, rw)
    counts = cnt.reshape(-1, SUBLANES, LANES)[:, 0, :N_EXPERTS].reshape(-1)
    wgt = jnp.swapaxes(w_gate.astype(BF16), 1, 2)
    wut = jnp.swapaxes(w_up.astype(BF16), 1, 2)
    wdt = jnp.swapaxes(w_down.astype(BF16), 1, 2)
    return _moe(counts, ht, rankt, rank, gate, x, mod, wgt, wut, wdt)


def _final_kernel(x_ref, g_ref, prompt_ref, sample_ref, *, prompt_tiles):
    x = x_ref[...]
    ms = jnp.mean(x * x, axis=-1, keepdims=True)
    y = x * lax.rsqrt(ms + NORM_EPS) * g_ref[...]
    i = pl.program_id(0)

    @pl.when(i < prompt_tiles)
    def _():
        prompt_ref[...] = y

    @pl.when(i >= prompt_tiles)
    def _():
        sample_ref[...] = y


def _final_norm(x, gain):
    tm = 1024
    prompt_tiles = N_PROMPT // tm
    return pl.pallas_call(
        functools.partial(_final_kernel, prompt_tiles=prompt_tiles),
        grid=(N_TOK // tm,),
        in_specs=[pl.BlockSpec((tm, D_MODEL), lambda i: (i, 0)),
                  pl.BlockSpec((1, D_MODEL), lambda i: (0, 0))],
        out_specs=[pl.BlockSpec((tm, D_MODEL), lambda i: (jnp.minimum(i, prompt_tiles - 1), 0)),
                   pl.BlockSpec((tm, D_MODEL), lambda i: (jnp.maximum(i - prompt_tiles, 0), 0))],
        out_shape=[jax.ShapeDtypeStruct((N_PROMPT, D_MODEL), F32),
                   jax.ShapeDtypeStruct((N_SAMPLE, D_MODEL), F32)],
        compiler_params=_params(("arbitrary",)),
        name="final_norm",
    )(x, gain)


def _rope_tables():
    pos = jnp.arange(DEC_SEQ, dtype=jnp.int32)
    d = jnp.arange(LANES, dtype=jnp.int32) % DIFF_HEAD_DIM

    def angles(p, theta, idx, half):
        inv_freq = jnp.exp(-math.log(theta) * jnp.arange(half, dtype=F32) / half)
        return p.astype(F32)[:, None] * inv_freq[idx][None, :]

    half_d = DIFF_ROT_DIM // 2
    ang = angles(pos, ROPE_THETA, d % half_d, half_d)
    active = (d < DIFF_ROT_DIM)[None, :]
    first = (d < half_d)[None, :]
    tab_d = jnp.stack([jnp.where(active, jnp.cos(ang), 1.0),
                       jnp.where(first, -jnp.sin(ang), 0.0),
                       jnp.where(active & ~first, jnp.sin(ang), 0.0)])

    half_g = GQA_HEAD_DIM // 4
    use_row = (d < GQA_HEAD_DIM // 2)[None, :]
    ang_r = angles(pos // GRID_W, AXIAL_THETA, d % half_g, half_g)
    ang_c = angles(pos % GRID_W, AXIAL_THETA, d % half_g, half_g)
    ang = jnp.where(use_row, ang_r, ang_c)
    first = ((d % (2 * half_g)) < half_g)[None, :]
    tab_g = jnp.stack([jnp.cos(ang),
                       jnp.where(first, -jnp.sin(ang), 0.0),
                       jnp.where(first, 0.0, jnp.sin(ang))])
    return tab_d.astype(F32), tab_g.astype(F32)


def _extend_w_in(w):
    def dup(cols):
        c = cols.reshape(D_MODEL, GQA_KV_HEADS, 1, GQA_HEAD_DIM)
        return jnp.broadcast_to(c, (D_MODEL, GQA_KV_HEADS, 2, GQA_HEAD_DIM)).reshape(D_MODEL, -1)
    kv = GQA_KV_HEADS * GQA_HEAD_DIM
    gk = w[:, COL_GK:COL_GK + kv]
    gv = w[:, COL_GK + kv:COL_GK + 2 * kv]
    return jnp.concatenate([w[:, :COL_GK], dup(gk), dup(gv), w[:, COL_GK + 2 * kv:]], axis=1)


def kernel(x_prompt, x_sample, c_prompt, c_sample, w_ada, b_ada, norm1, w_in, diff_lambda, diff_subln,
           gqa_q_norm, gqa_k_norm, w_out, norm2, ffn_w_gate, ffn_w_up, ffn_w_down, router_w,
           moe_w_gate, moe_w_up, moe_w_down, final_norm):
    x = jnp.concatenate([x_prompt.reshape(N_PROMPT, D_MODEL), x_sample.reshape(N_SAMPLE, D_MODEL)], axis=0)
    n_c = BATCH + DEC_BATCH
    c_all = jnp.concatenate([c_prompt, c_sample, jnp.zeros((16 - n_c, D_MODEL), F32)], axis=0)
    mod_all = _ada(c_all, w_ada, b_ada)
    chunk_src = jnp.asarray(list(range(BATCH)) + [BATCH + i // 2 for i in range(2 * DEC_BATCH)], jnp.int32)
    tab_d, tab_g = _rope_tables()
    grp = jnp.kron(jnp.eye(MXU_COLS // GQA_HEAD_DIM, dtype=F32),
                   jnp.full((GQA_HEAD_DIM, GQA_HEAD_DIM), 1.0 / GQA_HEAD_DIM, F32)).astype(BF16)

    for l in range(DEPTH):
        mod = mod_all[l][chunk_src].reshape(N_CHUNKS, N_MOD, D_MODEL)
        lam_init = 0.8 - 0.6 * math.exp(-0.3 * l)
        w_ext = _extend_w_in(w_in[l]).astype(BF16)
        gains = jnp.stack([jnp.tile(gqa_q_norm[l], PROJ_TN // GQA_HEAD_DIM),
                           jnp.tile(gqa_k_norm[l], PROJ_TN // GQA_HEAD_DIM)])
        proj = _proj(x, mod, norm1[l][None, :], w_ext, tab_d, tab_g, gains, grp)
        subln = diff_subln[l][None, :]
        a, b = None, None
        for batch, seq, row0 in ((BATCH, SEQ, 0), (DEC_BATCH, DEC_SEQ, N_PROMPT)):
            a = _attention(proj, diff=True, batch=batch, seq=seq, row0=row0, lam_init=lam_init,
                           lam=diff_lambda[l], subln=subln, out=a)
            b = _attention(proj, diff=False, batch=batch, seq=seq, row0=row0, out=b)
        x = _merge(proj, a, b, x, mod, w_out[l].astype(BF16))
        if l % 2 == 0:
            x = _ffn(x, mod, norm2[l][None, :], ffn_w_gate[l // 2].astype(BF16),
                     ffn_w_up[l // 2].astype(BF16), ffn_w_down[l // 2].astype(BF16))
        else:
            x = _moe_layer(x, mod, norm2[l][None, :], router_w[l // 2], moe_w_gate[l // 2],
                           moe_w_up[l // 2], moe_w_down[l // 2])
    y_prompt, y_sample = _final_norm(x, final_norm[None, :])
    return (y_prompt.reshape(BATCH, SEQ, D_MODEL), y_sample.reshape(DEC_BATCH, DEC_SEQ, D_MODEL))
```

```python
import functools
import math

import jax
import jax.numpy as jnp
from jax import lax
from jax.experimental import pallas as pl
from jax.experimental.pallas import tpu as pltpu

F32 = jnp.float32
BF16 = jnp.bfloat16

D_MODEL = 1024
BATCH = 8
SEQ = 4096
DEPTH = 4
DEC_BATCH = 2
DEC_SEQ = 8192
GRID_W = 64
DIFF_HEADS = 8
DIFF_HEAD_DIM = 64
DIFF_ROT_DIM = DIFF_HEAD_DIM // 4
ROPE_THETA = 500000.0
GQA_Q_HEADS = 16
GQA_KV_HEADS = 4
GQA_HEAD_DIM = 64
AXIAL_THETA = 10000.0
D_FF_DENSE = 2816
N_EXPERTS = 8
D_FF_EXPERT = 3584
NORM_EPS = 1e-6
N_MOD = 6

N_PROMPT = BATCH * SEQ
N_SAMPLE = DEC_BATCH * DEC_SEQ
N_TOK = N_PROMPT + N_SAMPLE
MOD_CHUNK = SEQ
N_CHUNKS = N_TOK // MOD_CHUNK

LANES = 128
SUBLANES = 8
MXU_COLS = 256
VMEM_LIMIT = 56 * 1024 * 1024

Q_SCALE = DIFF_HEAD_DIM ** -0.5 * math.log2(math.e)

PROJ_TN = 1024
PROJ_SEG = 512
COL_DQ, COL_DK, COL_DV, COL_GQ = 0, 1024, 2048, 3072
COL_GK, COL_GV, COL_GA, COL_GB = 4096, 4608, 5120, 6144
PROJ_W = 7168
N_PROJ_TILES = PROJ_W // PROJ_TN

NT_DIMS = (((1,), (1,)), ((), ()))


def _params(sem, vmem=VMEM_LIMIT):
    return pltpu.CompilerParams(dimension_semantics=sem, vmem_limit_bytes=vmem)


def _sigmoid(x):
    return 0.5 * jnp.tanh(0.5 * x) + 0.5


def _ada_kernel(c_ref, w_ref, b_ref, o_ref):
    c = c_ref[...]
    ca = c * _sigmoid(c)
    o_ref[0] = jnp.dot(ca.astype(BF16), w_ref[0].astype(BF16), preferred_element_type=F32) + b_ref[0]


def _ada(c_all, w_ada, b_ada):
    rows = c_all.shape[0]
    tn = 1536
    width = N_MOD * D_MODEL
    return pl.pallas_call(
        _ada_kernel,
        grid=(DEPTH, width // tn),
        in_specs=[pl.BlockSpec((rows, D_MODEL), lambda l, j: (0, 0)),
                  pl.BlockSpec((1, D_MODEL, tn), lambda l, j: (l, 0, j)),
                  pl.BlockSpec((1, 1, tn), lambda l, j: (l, 0, j))],
        out_specs=pl.BlockSpec((1, rows, tn), lambda l, j: (l, 0, j)),
        out_shape=jax.ShapeDtypeStruct((DEPTH, rows, width), F32),
        compiler_params=_params(("arbitrary", "arbitrary")),
        name="ada_mod",
    )(c_all, w_ada, b_ada.reshape(DEPTH, 1, width))


def _modulated_norm(x, gain, shift, scale):
    ms = jnp.mean(x * x, axis=-1, keepdims=True)
    y = x * lax.rsqrt(ms + NORM_EPS) * gain
    return y * (1.0 + scale) + shift


def _rotate(x, c, sm, sp, shift):
    return x * c + pltpu.roll(x, LANES - shift, axis=1) * sm + pltpu.roll(x, shift, axis=1) * sp


def _pos_block(i, tm):
    return jnp.where(i < N_PROMPT // tm, i % (SEQ // tm), i % (DEC_SEQ // tm))


def _proj_kernel(x_ref, mod_ref, n1_ref, w_ref, tabd_ref, tabg_ref, gain_ref, grp_ref, o_ref, h_scr, acc_a, acc_b):
    j = pl.program_id(1)

    def matmul(dst):
        dst[...] = jnp.dot(h_scr[...], w_ref[...], preferred_element_type=F32)

    def rope_epilogue(src, lo, hi, is_q):
        c, sm, sp = tabd_ref[0], tabd_ref[1], tabd_ref[2]
        for col in range(lo, hi, LANES):
            xs = src[:, col:col + LANES]
            if is_q:
                xs = xs * Q_SCALE
            o_ref[:, col:col + LANES] = _rotate(xs, c, sm, sp, DIFF_ROT_DIM // 2).astype(BF16)

    def plain_epilogue(src, lo, hi):
        o_ref[:, lo:hi] = src[:, lo:hi].astype(BF16)

    def norm_rope_epilogue(src, lo, hi, is_q):
        row = 0 if is_q else 1
        c, sm, sp = tabg_ref[0], tabg_ref[1], tabg_ref[2]
        for t in range(lo, hi, MXU_COLS):
            a = src[:, t:t + MXU_COLS]
            sq = a * a
            hi_part = sq.astype(BF16)
            lo_part = (sq - hi_part.astype(F32)).astype(BF16)
            ms = (jnp.dot(hi_part, grp_ref[...], preferred_element_type=F32)
                  + jnp.dot(lo_part, grp_ref[...], preferred_element_type=F32))
            gain = gain_ref[row:row + 1, :] * (Q_SCALE if is_q else 1.0)
            y = a * lax.rsqrt(ms + NORM_EPS) * gain
            for s in range(MXU_COLS // LANES):
                ys = y[:, s * LANES:(s + 1) * LANES]
                o_ref[:, t + s * LANES:t + (s + 1) * LANES] = _rotate(ys, c, sm, sp, GQA_HEAD_DIM // 4).astype(BF16)

    def gate_epilogue(src, lo, hi):
        for col in range(lo, hi, LANES):
            o_ref[:, col:col + LANES] = _sigmoid(src[:, col:col + LANES]).astype(BF16)

    def epilogue(t, src):
        base = t * PROJ_TN
        for lo in range(0, PROJ_TN, PROJ_SEG):
            col = base + lo
            if col < COL_DV:
                rope_epilogue(src, lo, lo + PROJ_SEG, col < COL_DK)
            elif col < COL_GQ or COL_GV <= col < COL_GA:
                plain_epilogue(src, lo, lo + PROJ_SEG)
            elif col < COL_GV:
                norm_rope_epilogue(src, lo, lo + PROJ_SEG, col < COL_GK)
            else:
                gate_epilogue(src, lo, lo + PROJ_SEG)

    @pl.when(j == 0)
    def _():
        h = _modulated_norm(x_ref[...], n1_ref[...], mod_ref[0, 0:1, :], mod_ref[0, 1:2, :])
        h_scr[...] = h.astype(BF16)
        matmul(acc_a)

    for t in range(N_PROJ_TILES):
        cur, prev = (acc_a, acc_b) if (t + 1) % 2 == 0 else (acc_b, acc_a)

        @pl.when(j == t + 1)
        def _(t=t, cur=cur, prev=prev):
            epilogue(t, prev)
            if t + 1 < N_PROJ_TILES:
                matmul(cur)


def _proj(x, mod, norm1, w_ext, tab_d, tab_g, gains, grp):
    tm = 1024
    last = N_PROJ_TILES - 1
    return pl.pallas_call(
        _proj_kernel,
        grid=(N_TOK // tm, N_PROJ_TILES + 1),
        in_specs=[pl.BlockSpec((tm, D_MODEL), lambda i, j: (i, 0)),
                  pl.BlockSpec((1, N_MOD, D_MODEL), lambda i, j: (i * tm // MOD_CHUNK, 0, 0)),
                  pl.BlockSpec((1, D_MODEL), lambda i, j: (0, 0)),
                  pl.BlockSpec((D_MODEL, PROJ_TN), lambda i, j: (0, jnp.minimum(j, last))),
                  pl.BlockSpec((3, tm, LANES), lambda i, j: (0, _pos_block(i, tm), 0)),
                  pl.BlockSpec((3, tm, LANES), lambda i, j: (0, _pos_block(i, tm), 0)),
                  pl.BlockSpec((2, MXU_COLS), lambda i, j: (0, 0)),
                  pl.BlockSpec((MXU_COLS, MXU_COLS), lambda i, j: (0, 0))],
        out_specs=pl.BlockSpec((tm, PROJ_TN), lambda i, j: (i, jnp.maximum(j - 1, 0))),
        out_shape=jax.ShapeDtypeStruct((N_TOK, PROJ_W), BF16),
        scratch_shapes=[pltpu.VMEM((tm, D_MODEL), BF16), pltpu.VMEM((tm, PROJ_TN), F32),
                        pltpu.VMEM((tm, PROJ_TN), F32)],
        compiler_params=_params(("arbitrary", "arbitrary")),
        name="proj_in",
    )(x, mod, norm1, w_ext, tab_d, tab_g, gains, grp)


ATT_SCORE_BYTES = 8 * 1024 * 1024


def _attn_kernel(*refs, diff, lam_init, nq, aliased):
    q_ref, k_ref, v_ref = refs[:3]
    if diff:
        lam_ref, subln_ref = refs[3:5]
    n_in = 3 + (2 if diff else 0) + (1 if aliased else 0)
    o_ref, bufs, vx_ref = refs[n_in], refs[n_in + 1:-1], refs[-1]
    set_a, set_b = bufs[:4], bufs[4:]
    half = LANES // 2
    step = pl.program_id(0)

    @pl.when(step == 0)
    def _():
        for ref in set_b:
            ref[...] = jnp.zeros_like(ref)

    @pl.when(jnp.maximum(step - 1, 0) % nq == 0)
    def _():
        v = v_ref[...]
        if diff:
            vx_ref[:, :LANES] = v
            vx_ref[:, LANES:] = jnp.ones_like(v)
        else:
            lane = lax.broadcasted_iota(jnp.int32, v.shape, 1)
            vx_ref[...] = jnp.where(lane < half, v, jnp.ones_like(v))

    def finish(s_ref, m_ref):
        m = jnp.max(m_ref[...], axis=-1, keepdims=True)
        p = jnp.exp2((s_ref[...] - m).astype(BF16))
        acc = jnp.dot(p, vx_ref[...], preferred_element_type=F32)
        return acc[:, :LANES] / acc[:, LANES:] if diff else acc

    def start(qh, s_ref, m_ref):
        s = lax.dot_general(qh, k_ref[...], NT_DIMS, preferred_element_type=F32)
        s_ref[...] = s
        top = s[:, :LANES]
        for u in range(1, s.shape[1] // LANES):
            top = jnp.maximum(top, s[:, u * LANES:(u + 1) * LANES])
        m_ref[...] = top

    def run(cur, prev):
        q = q_ref[...]
        low = lax.broadcasted_iota(jnp.int32, q.shape, 1) < half
        zero = jnp.zeros_like(q)
        acc_lo = finish(prev[0], prev[2])
        start(jnp.where(low, q, zero), cur[0], cur[2])
        acc_hi = finish(prev[1], prev[3])
        if diff:
            lp = lam_ref[...]
            t1 = jnp.sum(lp[0:1] * lp[1:2], axis=-1, keepdims=True)
            t2 = jnp.sum(lp[2:3] * lp[3:4], axis=-1, keepdims=True)
            lam = jnp.exp(t1) - jnp.exp(t2) + lam_init
            a = acc_lo - lam * acc_hi
            ms = jnp.mean(a * a, axis=-1, keepdims=True)
            a = a * lax.rsqrt(ms + NORM_EPS) * subln_ref[...] * (1.0 - lam_init)
            o_ref[...] = a.astype(BF16)
        else:
            out = jnp.where(low, acc_lo / pltpu.roll(acc_lo, half, axis=1),
                            pltpu.roll(acc_hi, half, axis=1) / acc_hi)
            o_ref[...] = out.astype(BF16)
        start(jnp.where(low, zero, q), cur[1], cur[3])

    @pl.when(step % 2 == 0)
    def _():
        run(set_a, set_b)

    @pl.when(step % 2 == 1)
    def _():
        run(set_b, set_a)


def _attention(proj, *, diff, batch, seq, row0, lam_init=0.0, lam=None, subln=None, out=None):
    tq = ATT_SCORE_BYTES // (4 * seq)
    nq = seq // tq
    q0 = row0 // tq
    s0 = row0 // seq
    n_units = DIFF_HEADS
    n_blocks = batch * n_units * nq
    kv_unit = (lambda h: h) if diff else (lambda h: h // 2)
    qc, kc, vc = ((COL_DQ, COL_DK, COL_DV) if diff else (COL_GQ, COL_GK, COL_GV))

    def block(g):
        return g // (n_units * nq), (g // nq) % n_units, g % nq

    def started(g):
        return block(jnp.minimum(g, n_blocks - 1))

    def finished(g):
        return block(jnp.maximum(g - 1, 0))

    def q_map(g):
        b, h, i = started(g)
        return q0 + b * nq + i, qc // LANES + h

    def k_map(g):
        b, h, _ = started(g)
        return s0 + b, kc // LANES + kv_unit(h)

    def v_map(g):
        b, h, _ = finished(g)
        return s0 + b, vc // LANES + kv_unit(h)

    def o_map(g):
        b, h, i = finished(g)
        return q0 + b * nq + i, h

    in_specs = [pl.BlockSpec((tq, LANES), q_map),
                pl.BlockSpec((seq, LANES), k_map),
                pl.BlockSpec((seq, LANES), v_map)]
    args = [proj, proj, proj]
    if diff:
        in_specs += [pl.BlockSpec((4, DIFF_HEAD_DIM), lambda g: (0, 0)),
                     pl.BlockSpec((1, LANES), lambda g: (0, 0))]
        args += [lam, subln]
    aliases = {}
    if out is not None:
        aliases = {len(args): 0}
        in_specs.append(pl.BlockSpec(memory_space=pl.ANY))
        args.append(out)
    return pl.pallas_call(
        functools.partial(_attn_kernel, diff=diff, lam_init=lam_init, nq=nq, aliased=out is not None),
        grid=(n_blocks + 1,),
        in_specs=in_specs,
        out_specs=pl.BlockSpec((tq, LANES), o_map),
        out_shape=jax.ShapeDtypeStruct((N_TOK, D_MODEL), BF16),
        input_output_aliases=aliases,
        scratch_shapes=(([pltpu.VMEM((tq, seq), F32)] * 2 + [pltpu.VMEM((tq, LANES), F32)] * 2) * 2
                        + [pltpu.VMEM((seq, 2 * LANES if diff else LANES), BF16)]),
        compiler_params=_params(("arbitrary",)),
        name=("diff_attn" if diff else "gqa_attn") + f"_{seq}",
    )(*args)


def _merge_kernel(ga_ref, gb_ref, a_ref, b_ref, x_ref, mod_ref, w_ref, o_ref):
    merged = (ga_ref[...].astype(F32) * a_ref[...].astype(F32)
              + gb_ref[...].astype(F32) * b_ref[...].astype(F32))
    y = jnp.dot(merged.astype(BF16), w_ref[...], preferred_element_type=F32)
    o_ref[...] = x_ref[...] + mod_ref[0, 2:3, :] * y


def _merge(proj, a, b, x, mod, w_out):
    tm = 512
    row = lambda i: (i, 0)
    return pl.pallas_call(
        _merge_kernel,
        grid=(N_TOK // tm,),
        in_specs=[pl.BlockSpec((tm, D_MODEL), lambda i: (i, COL_GA // D_MODEL)),
                  pl.BlockSpec((tm, D_MODEL), lambda i: (i, COL_GB // D_MODEL)),
                  pl.BlockSpec((tm, D_MODEL), row),
                  pl.BlockSpec((tm, D_MODEL), row),
                  pl.BlockSpec((tm, D_MODEL), row),
                  pl.BlockSpec((1, N_MOD, D_MODEL), lambda i: (i * tm // MOD_CHUNK, 0, 0)),
                  pl.BlockSpec((D_MODEL, D_MODEL), lambda i: (0, 0))],
        out_specs=pl.BlockSpec((tm, D_MODEL), row),
        out_shape=jax.ShapeDtypeStruct((N_TOK, D_MODEL), F32),
        compiler_params=_params(("arbitrary",)),
        name="merge_out",
    )(proj, proj, a, b, x, mod, w_out)


def _ffn_kernel(x_ref, mod_ref, n2_ref, wg_ref, wu_ref, wd_ref, o_ref, h_scr):
    f = pl.program_id(1)

    @pl.when(f == 0)
    def _():
        h = _modulated_norm(x_ref[...], n2_ref[...], mod_ref[0, 3:4, :], mod_ref[0, 4:5, :])
        h_scr[...] = h.astype(BF16)
        o_ref[...] = jnp.zeros_like(o_ref)

    h = h_scr[...]
    g = jnp.dot(h, wg_ref[...], preferred_element_type=F32)
    u = jnp.dot(h, wu_ref[...], preferred_element_type=F32)
    act = (g * _sigmoid(g) * u).astype(BF16)
    o_ref[...] += jnp.dot(act, wd_ref[...], preferred_element_type=F32)

    @pl.when(f == pl.num_programs(1) - 1)
    def _():
        o_ref[...] = x_ref[...] + mod_ref[0, 5:6, :] * o_ref[...]


def _ffn(x, mod, norm2, wg, wu, wd):
    tm, tf = 1024, 256
    return pl.pallas_call(
        _ffn_kernel,
        grid=(N_TOK // tm, D_FF_DENSE // tf),
        in_specs=[pl.BlockSpec((tm, D_MODEL), lambda i, f: (i, 0)),
                  pl.BlockSpec((1, N_MOD, D_MODEL), lambda i, f: (i * tm // MOD_CHUNK, 0, 0)),
                  pl.BlockSpec((1, D_MODEL), lambda i, f: (0, 0)),
                  pl.BlockSpec((D_MODEL, tf), lambda i, f: (0, f)),
                  pl.BlockSpec((D_MODEL, tf), lambda i, f: (0, f)),
                  pl.BlockSpec((tf, D_MODEL), lambda i, f: (f, 0))],
        out_specs=pl.BlockSpec((tm, D_MODEL), lambda i, f: (i, 0)),
        out_shape=jax.ShapeDtypeStruct((N_TOK, D_MODEL), F32),
        scratch_shapes=[pltpu.VMEM((tm, D_MODEL), BF16)],
        compiler_params=_params(("arbitrary", "arbitrary")),
        name="ffn_dense",
    )(x, mod, norm2, wg, wu, wd)


MOE_TM = 4 * MXU_COLS
MOE_FF_SUB = 896


def _tile_mod(mod_a_ref, mod_b_ref, row, tm, mod_chunk):
    i = pl.program_id(0)
    first_of_b = ((i + 1) * tm - 1) // mod_chunk * mod_chunk
    token = i * tm + lax.broadcasted_iota(jnp.int32, (tm, 1), 0)
    return jnp.where(token >= first_of_b, mod_b_ref[0, row:row + 1, :], mod_a_ref[0, row:row + 1, :])


def _router_kernel(x_ref, mod_a_ref, mod_b_ref, n2_ref, rw_ref, ht_ref, gate_ref, rank_ref, rankt_ref, cnt_ref,
                   *, mod_chunk):
    tm = x_ref.shape[0]
    h = _modulated_norm(x_ref[...], n2_ref[...], _tile_mod(mod_a_ref, mod_b_ref, 3, tm, mod_chunk),
                        _tile_mod(mod_a_ref, mod_b_ref, 4, tm, mod_chunk))
    ht_ref[...] = h.T.astype(BF16)
    logits = jnp.dot(h, rw_ref[...], preferred_element_type=F32, precision=lax.Precision.HIGHEST)
    lane = lax.broadcasted_iota(jnp.int32, logits.shape, 1).astype(F32)
    neg = jnp.float32(-jnp.inf)
    lg = jnp.where(lane < N_EXPERTS, logits, neg)
    m1 = jnp.max(lg, axis=-1, keepdims=True)
    i1 = jnp.min(jnp.where(lg == m1, lane, float(LANES)), axis=-1, keepdims=True)
    lg2 = jnp.where(lane == i1, neg, lg)
    m2 = jnp.max(lg2, axis=-1, keepdims=True)
    i2 = jnp.min(jnp.where(lg2 == m2, lane, float(LANES)), axis=-1, keepdims=True)
    e2 = jnp.exp(m2 - m1)
    den = 1.0 + e2
    gate_ref[...] = jnp.where(lane == i1, 1.0 / den, 0.0) + jnp.where(lane == i2, e2 / den, 0.0)
    sel = (lane == i1) | (lane == i2)
    sel_f = jnp.where(sel, 1.0, 0.0)
    before = (lax.broadcasted_iota(jnp.int32, (tm, tm), 1)
              < lax.broadcasted_iota(jnp.int32, (tm, tm), 0))
    slot = jnp.dot(jnp.where(before, 1.0, 0.0).astype(BF16), sel_f.astype(BF16), preferred_element_type=F32)
    rank = jnp.where(sel, slot, -1.0)
    rank_ref[...] = rank
    rankt_ref[...] = rank.T[:SUBLANES, :]
    cnt = jnp.sum(sel_f, axis=0, keepdims=True)
    cnt_ref[...] = jnp.broadcast_to(cnt, (SUBLANES, LANES)).astype(jnp.int32)


def _router(x, mod, norm2, rw, *, tm=MOE_TM, mod_chunk=MOD_CHUNK):
    n, d = x.shape
    nt = n // tm
    return pl.pallas_call(
        functools.partial(_router_kernel, mod_chunk=mod_chunk),
        grid=(nt,),
        in_specs=[pl.BlockSpec((tm, d), lambda i: (i, 0)),
                  pl.BlockSpec((1, N_MOD, d), lambda i: (i * tm // mod_chunk, 0, 0)),
                  pl.BlockSpec((1, N_MOD, d), lambda i: (((i + 1) * tm - 1) // mod_chunk, 0, 0)),
                  pl.BlockSpec((1, d), lambda i: (0, 0)),
                  pl.BlockSpec((d, LANES), lambda i: (0, 0))],
        out_specs=[pl.BlockSpec((d, tm), lambda i: (0, i)),
                   pl.BlockSpec((tm, LANES), lambda i: (i, 0)),
                   pl.BlockSpec((tm, LANES), lambda i: (i, 0)),
                   pl.BlockSpec((SUBLANES, tm), lambda i: (i, 0)),
                   pl.BlockSpec((SUBLANES, LANES), lambda i: (i, 0))],
        out_shape=[jax.ShapeDtypeStruct((d, n), BF16),
                   jax.ShapeDtypeStruct((n, LANES), F32),
                   jax.ShapeDtypeStruct((n, LANES), F32),
                   jax.ShapeDtypeStruct((nt * SUBLANES, tm), F32),
                   jax.ShapeDtypeStruct((nt * SUBLANES, LANES), jnp.int32)],
        compiler_params=_params(("arbitrary",)),
        name="moe_router",
    )(x, mod, mod, norm2, rw)


def _moe_kernel(cnt_ref, ht_ref, rankt_ref, rank_ref, gate_ref, x_ref, mod_a_ref, mod_b_ref, wg_ref, wu_ref, wd_ref,
                o_ref, xt_scr, yt_scr, *, mod_chunk):
    i, e, f = pl.program_id(0), pl.program_id(1), pl.program_id(2)
    last_f = pl.num_programs(2) - 1
    tm = o_ref.shape[0]
    n_blk = (cnt_ref[i * N_EXPERTS + e] + MXU_COLS - 1) // MXU_COLS

    @pl.when((e == 0) & (f == 0))
    def _():
        o_ref[...] = jnp.zeros_like(o_ref)

    @pl.when(f == 0)
    def _():
        slot = rankt_ref[pl.ds(e, 1), :].astype(jnp.int32)
        sub = lax.broadcasted_iota(jnp.int32, (MXU_COLS, tm), 0)

        def gather(b, carry):
            pick = jnp.where(sub + b * MXU_COLS == slot, 1.0, 0.0).astype(BF16)
            xt_scr[b] = lax.dot_general(ht_ref[...], pick, NT_DIMS, preferred_element_type=F32).astype(BF16)
            yt_scr[b] = jnp.zeros(yt_scr.shape[1:], F32)
            return carry

        lax.fori_loop(0, n_blk, gather, 0)

    def expert(b, carry):
        xb = xt_scr[b]
        g = jnp.dot(wg_ref[0], xb, preferred_element_type=F32)
        u = jnp.dot(wu_ref[0], xb, preferred_element_type=F32)
        act = (g * _sigmoid(g) * u).astype(BF16)
        yt_scr[b] += jnp.dot(wd_ref[0], act, preferred_element_type=F32)
        return carry

    lax.fori_loop(0, n_blk, expert, 0)

    @pl.when(f == last_f)
    def _():
        on_lane = lax.broadcasted_iota(jnp.int32, (tm, LANES), 1) == e
        slot = jnp.sum(jnp.where(on_lane, rank_ref[...], 0.0), axis=-1, keepdims=True).astype(jnp.int32)
        gate = jnp.sum(jnp.where(on_lane, gate_ref[...], 0.0), axis=-1, keepdims=True)
        col = lax.broadcasted_iota(jnp.int32, (tm, MXU_COLS), 1)

        def scatter(b, carry):
            pick = jnp.where(col + b * MXU_COLS == slot, 1.0, 0.0).astype(BF16)
            y = lax.dot_general(pick, yt_scr[b].astype(BF16), NT_DIMS, preferred_element_type=F32)
            o_ref[...] += gate * y
            return carry

        lax.fori_loop(0, n_blk, scatter, 0)

    @pl.when((e == pl.num_programs(1) - 1) & (f == last_f))
    def _():
        o_ref[...] = x_ref[...] + _tile_mod(mod_a_ref, mod_b_ref, 5, tm, mod_chunk) * o_ref[...]


def _moe(counts, ht, rankt, rank, gate, x, mod, wgt, wut, wdt, *, tm=MOE_TM, tf=MOE_FF_SUB, mod_chunk=MOD_CHUNK):
    n, d = x.shape
    n_exp, d_ff, _ = wgt.shape
    grid_spec = pltpu.PrefetchScalarGridSpec(
        num_scalar_prefetch=1,
        grid=(n // tm, n_exp, d_ff // tf),
        in_specs=[pl.BlockSpec((d, tm), lambda i, e, f, c: (0, i)),
                  pl.BlockSpec((SUBLANES, tm), lambda i, e, f, c: (i, 0)),
                  pl.BlockSpec((tm, LANES), lambda i, e, f, c: (i, 0)),
                  pl.BlockSpec((tm, LANES), lambda i, e, f, c: (i, 0)),
                  pl.BlockSpec((tm, d), lambda i, e, f, c: (i, 0)),
                  pl.BlockSpec((1, N_MOD, d), lambda i, e, f, c: (i * tm // mod_chunk, 0, 0)),
                  pl.BlockSpec((1, N_MOD, d), lambda i, e, f, c: (((i + 1) * tm - 1) // mod_chunk, 0, 0)),
                  pl.BlockSpec((1, tf, d), lambda i, e, f, c: (e, f, 0)),
                  pl.BlockSpec((1, tf, d), lambda i, e, f, c: (e, f, 0)),
                  pl.BlockSpec((1, d, tf), lambda i, e, f, c: (e, 0, f))],
        out_specs=pl.BlockSpec((tm, d), lambda i, e, f, c: (i, 0)),
        scratch_shapes=[pltpu.VMEM((tm // MXU_COLS, d, MXU_COLS), BF16),
                        pltpu.VMEM((tm // MXU_COLS, d, MXU_COLS), F32)])
    return pl.pallas_call(
        functools.partial(_moe_kernel, mod_chunk=mod_chunk),
        grid_spec=grid_spec,
        out_shape=jax.ShapeDtypeStruct((n, d), F32),
        compiler_params=_params(("arbitrary", "arbitrary", "arbitrary")),
        name="moe_experts",
    )(counts, ht, rankt, rank, gate, x, mod, mod, wgt, wut, wdt)


def _moe_layer(x, mod, norm2, router_w, w_gate, w_up, w_down):
    rw = jnp.pad(router_w, ((0, 0), (0, LANES - N_EXPERTS)))
    ht, gate, rank, rankt, cnt = _router(x, mod, norm2, rw)
    counts = cnt.reshape(-1, SUBLANES, LANES)[:, 0, :N_EXPERTS].reshape(-1)
    wgt = jnp.swapaxes(w_gate.astype(BF16), 1, 2)
    wut = jnp.swapaxes(w_up.astype(BF16), 1, 2)
    wdt = jnp.swapaxes(w_down.astype(BF16), 1, 2)
    return _moe(counts, ht, rankt, rank, gate, x, mod, wgt, wut, wdt)


def _final_kernel(x_ref, g_ref, prompt_ref, sample_ref, *, prompt_tiles):
    x = x_ref[...]
    ms = jnp.mean(x * x, axis=-1, keepdims=True)
    y = x * lax.rsqrt(ms + NORM_EPS) * g_ref[...]
    i = pl.program_id(0)

    @pl.when(i < prompt_tiles)
    def _():
        prompt_ref[...] = y

    @pl.when(i >= prompt_tiles)
    def _():
        sample_ref[...] = y


def _final_norm(x, gain):
    tm = 1024
    prompt_tiles = N_PROMPT // tm
    return pl.pallas_call(
        functools.partial(_final_kernel, prompt_tiles=prompt_tiles),
        grid=(N_TOK // tm,),
        in_specs=[pl.BlockSpec((tm, D_MODEL), lambda i: (i, 0)),
                  pl.BlockSpec((1, D_MODEL), lambda i: (0, 0))],
        out_specs=[pl.BlockSpec((tm, D_MODEL), lambda i: (jnp.minimum(i, prompt_tiles - 1), 0)),
                   pl.BlockSpec((tm, D_MODEL), lambda i: (jnp.maximum(i - prompt_tiles, 0), 0))],
        out_shape=[jax.ShapeDtypeStruct((N_PROMPT, D_MODEL), F32),
                   jax.ShapeDtypeStruct((N_SAMPLE, D_MODEL), F32)],
        compiler_params=_params(("arbitrary",)),
        name="final_norm",
    )(x, gain)


def _rope_tables():
    pos = jnp.arange(DEC_SEQ, dtype=jnp.int32)
    d = jnp.arange(LANES, dtype=jnp.int32) % DIFF_HEAD_DIM

    def angles(p, theta, idx, half):
        inv_freq = jnp.exp(-math.log(theta) * jnp.arange(half, dtype=F32) / half)
        return p.astype(F32)[:, None] * inv_freq[idx][None, :]

    half_d = DIFF_ROT_DIM // 2
    ang = angles(pos, ROPE_THETA, d % half_d, half_d)
    active = (d < DIFF_ROT_DIM)[None, :]
    first = (d < half_d)[None, :]
    tab_d = jnp.stack([jnp.where(active, jnp.cos(ang), 1.0),
                       jnp.where(first, -jnp.sin(ang), 0.0),
                       jnp.where(active & ~first, jnp.sin(ang), 0.0)])

    half_g = GQA_HEAD_DIM // 4
    use_row = (d < GQA_HEAD_DIM // 2)[None, :]
    ang_r = angles(pos // GRID_W, AXIAL_THETA, d % half_g, half_g)
    ang_c = angles(pos % GRID_W, AXIAL_THETA, d % half_g, half_g)
    ang = jnp.where(use_row, ang_r, ang_c)
    first = ((d % (2 * half_g)) < half_g)[None, :]
    tab_g = jnp.stack([jnp.cos(ang),
                       jnp.where(first, -jnp.sin(ang), 0.0),
                       jnp.where(first, 0.0, jnp.sin(ang))])
    return tab_d.astype(F32), tab_g.astype(F32)


def _extend_w_in(w):
    def dup(cols):
        c = cols.reshape(D_MODEL, GQA_KV_HEADS, 1, GQA_HEAD_DIM)
        return jnp.broadcast_to(c, (D_MODEL, GQA_KV_HEADS, 2, GQA_HEAD_DIM)).reshape(D_MODEL, -1)
    kv = GQA_KV_HEADS * GQA_HEAD_DIM
    gk = w[:, COL_GK:COL_GK + kv]
    gv = w[:, COL_GK + kv:COL_GK + 2 * kv]
    return jnp.concatenate([w[:, :COL_GK], dup(gk), dup(gv), w[:, COL_GK + 2 * kv:]], axis=1)


def kernel(x_prompt, x_sample, c_prompt, c_sample, w_ada, b_ada, norm1, w_in, diff_lambda, diff_subln,
           gqa_q_norm, gqa_k_norm, w_out, norm2, ffn_w_gate, ffn_w_up, ffn_w_down, router_w,
           moe_w_gate, moe_w_up, moe_w_down, final_norm):
    x = jnp.concatenate([x_prompt.reshape(N_PROMPT, D_MODEL), x_sample.reshape(N_SAMPLE, D_MODEL)], axis=0)
    n_c = BATCH + DEC_BATCH
    c_all = jnp.concatenate([c_prompt, c_sample, jnp.zeros((16 - n_c, D_MODEL), F32)], axis=0)
    mod_all = _ada(c_all, w_ada, b_ada)
    chunk_src = jnp.asarray(list(range(BATCH)) + [BATCH + i // 2 for i in range(2 * DEC_BATCH)], jnp.int32)
    tab_d, tab_g = _rope_tables()
    grp = jnp.kron(jnp.eye(MXU_COLS // GQA_HEAD_DIM, dtype=F32),
                   jnp.full((GQA_HEAD_DIM, GQA_HEAD_DIM), 1.0 / GQA_HEAD_DIM, F32)).astype(BF16)

    for l in range(DEPTH):
        mod = mod_all[l][chunk_src].reshape(N_CHUNKS, N_MOD, D_MODEL)
        lam_init = 0.8 - 0.6 * math.exp(-0.3 * l)
        w_ext = _extend_w_in(w_in[l]).astype(BF16)
        gains = jnp.stack([jnp.tile(gqa_q_norm[l], MXU_COLS // GQA_HEAD_DIM),
                           jnp.tile(gqa_k_norm[l], MXU_COLS // GQA_HEAD_DIM)])
        proj = _proj(x, mod, norm1[l][None, :], w_ext, tab_d, tab_g, gains, grp)
        subln = diff_subln[l][None, :]
        a, b = None, None
        for batch, seq, row0 in ((BATCH, SEQ, 0), (DEC_BATCH, DEC_SEQ, N_PROMPT)):
            a = _attention(proj, diff=True, batch=batch, seq=seq, row0=row0, lam_init=lam_init,
                           lam=diff_lambda[l], subln=subln, out=a)
            b = _attention(proj, diff=False, batch=batch, seq=seq, row0=row0, out=b)
        x = _merge(proj, a, b, x, mod, w_out[l].astype(BF16))
        if l % 2 == 0:
            x = _ffn(x, mod, norm2[l][None, :], ffn_w_gate[l // 2].astype(BF16),
                     ffn_w_up[l // 2].astype(BF16), ffn_w_down[l // 2].astype(BF16))
        else:
            x = _moe_layer(x, mod, norm2[l][None, :], router_w[l // 2], moe_w_gate[l // 2],
                           moe_w_up[l // 2], moe_w_down[l // 2])
    y_prompt, y_sample = _final_norm(x, final_norm[None, :])
    return (y_prompt.reshape(BATCH, SEQ, D_MODEL), y_sample.reshape(DEC_BATCH, DEC_SEQ, D_MODEL))
```
